```python
import jax
import jax.numpy as jnp
from jax import lax
import numpy as np

D_MODEL = 1024
BATCH = 1
SEQ = 16384
DEPTH = 2

GRID_W = 64
CTX_LEN = 256
EPS = 1e-6
HEAD_DIM = 64
N_Q_HEADS = 8
N_KV_HEADS = 2
Q_PER_KV = N_Q_HEADS // N_KV_HEADS
ATTN_WIDTH = N_Q_HEADS * HEAD_DIM
KV_WIDTH = N_KV_HEADS * HEAD_DIM
WINDOW = 128
ATTN_BLOCK = 128
ROPE_THETA = 10000.0
AXIS_DIM = HEAD_DIM // 2
HG_HEADS = 4
HG_K = 64
HG_V = 64
HG_WIDTH = HG_HEADS * HG_V
HG_CHUNK = 64
GM_GROUPS = 4
GM_DIM = 64
GM_WIDTH = GM_GROUPS * GM_DIM
GM_CHUNK = 128
MIX_WIDTH = ATTN_WIDTH + HG_WIDTH + GM_WIDTH
IN_SPLITS = (ATTN_WIDTH, KV_WIDTH, KV_WIDTH, HG_WIDTH, HG_WIDTH, HG_WIDTH, HG_WIDTH, HG_WIDTH, GM_WIDTH, GM_WIDTH)
IN_WIDTH = ATTN_WIDTH + 2 * KV_WIDTH + 5 * HG_WIDTH + 2 * GM_WIDTH
FFN_DIM = 2816
N_EXPERTS = 8
TOP_K = 2
EXPERT_DIM = 3584
N_DENSE = (DEPTH + 1) // 2
N_MOE = DEPTH // 2

kernel_name = 'hybrid_hgrn2_swa_gmlp_moe_dit_block'


def _rms(x):
    xf = x.astype(jnp.float32)
    return (xf * lax.rsqrt(jnp.mean(xf * xf, axis=-1, keepdims=True) + EPS)).astype(x.dtype)


def _heads(t, n):
    return t.reshape(t.shape[:-1] + (n, t.shape[-1] // n))


def _split_cols(p):
    out, start = [], 0
    for w in IN_SPLITS:
        out.append(p[..., start:start + w])
        start += w
    return out


def _axial_rope_tables(rows, dtype):
    row = jnp.repeat(jnp.arange(rows, dtype=jnp.float32), GRID_W, total_repeat_length=rows * GRID_W)
    col = jnp.tile(jnp.arange(GRID_W, dtype=jnp.float32), rows)
    inv_freq = ROPE_THETA ** (-jnp.arange(0, AXIS_DIM, 2, dtype=jnp.float32) / AXIS_DIM)
    ang_r = row[:, None] * inv_freq
    ang_c = col[:, None] * inv_freq
    return tuple(t.astype(dtype) for t in (jnp.cos(ang_r), jnp.sin(ang_r), jnp.cos(ang_c), jnp.sin(ang_c)))


def _rope_half(x, cos, sin):
    x1, x2 = jnp.split(x, 2, axis=-1)
    cos = cos[:, None, :]
    sin = sin[:, None, :]
    return jnp.concatenate([x1 * cos - x2 * sin, x1 * sin + x2 * cos], axis=-1)


def _apply_axial_rope(x, tables):
    cos_r, sin_r, cos_c, sin_c = tables
    return jnp.concatenate([_rope_half(x[..., :AXIS_DIM], cos_r, sin_r),
                            _rope_half(x[..., AXIS_DIM:], cos_c, sin_c)], axis=-1)


def _window_attention(q, k, v, k_ctx, v_ctx, sink):
    b, s = q.shape[:2]
    nb = s // ATTN_BLOCK
    qb = q.reshape(b, nb, ATTN_BLOCK, N_KV_HEADS, Q_PER_KV, HEAD_DIM) * (HEAD_DIM ** -0.5)
    pad = ((0, 0), (ATTN_BLOCK, ATTN_BLOCK), (0, 0), (0, 0))

    def band_blocks(t):
        tp = jnp.pad(t, pad).reshape(b, nb + 2, ATTN_BLOCK, N_KV_HEADS, HEAD_DIM)
        return jnp.concatenate([tp[:, :-2], tp[:, 1:-1], tp[:, 2:]], axis=2)

    kw, vw = band_blocks(k), band_blocks(v)
    s_win = jnp.einsum('bnqhgd,bnkhd->bnhgqk', qb, kw).astype(jnp.float32)
    a = jnp.arange(ATTN_BLOCK)[:, None]
    r = jnp.arange(3 * ATTN_BLOCK)[None, :]
    band = jnp.abs(a + ATTN_BLOCK - r) <= WINDOW
    key_pos = jnp.arange(nb)[:, None] * ATTN_BLOCK - ATTN_BLOCK + jnp.arange(3 * ATTN_BLOCK)[None, :]
    in_range = (key_pos >= 0) & (key_pos < s)
    valid = band[None] & in_range[:, None, :]
    s_win = jnp.where(valid[None, :, None, None], s_win, -jnp.inf)
    s_ctx = jnp.einsum('bnqhgd,bchd->bnhgqc', qb, k_ctx).astype(jnp.float32)
    s_sink = jnp.broadcast_to(sink.astype(jnp.float32).reshape(N_KV_HEADS, Q_PER_KV)[None, None, :, :, None, None],
                              s_win.shape[:-1] + (1,))
    p = jax.nn.softmax(jnp.concatenate([s_win, s_ctx, s_sink], axis=-1), axis=-1)
    n_win = 3 * ATTN_BLOCK
    n_ctx = k_ctx.shape[1]
    p_win = p[..., :n_win].astype(v.dtype)
    p_ctx = p[..., n_win:n_win + n_ctx].astype(v.dtype)
    o = (jnp.einsum('bnhgqk,bnkhd->bnqhgd', p_win, vw)
         + jnp.einsum('bnhgqc,bchd->bnqhgd', p_ctx, v_ctx))
    return o.reshape(b, s, ATTN_WIDTH)


def _context_attention(q, k, v, sink):
    b, n = q.shape[:2]
    qg = q.reshape(b, n, N_KV_HEADS, Q_PER_KV, HEAD_DIM) * (HEAD_DIM ** -0.5)
    sc = jnp.einsum('bqhgd,bkhd->bhgqk', qg, k).astype(jnp.float32)
    sk = jnp.broadcast_to(sink.astype(jnp.float32).reshape(N_KV_HEADS, Q_PER_KV)[None, :, :, None, None],
                          sc.shape[:-1] + (1,))
    p = jax.nn.softmax(jnp.concatenate([sc, sk], axis=-1), axis=-1)[..., :-1]
    o = jnp.einsum('bhgqk,bkhd->bqhgd', p.astype(v.dtype), v)
    return o.reshape(b, n, ATTN_WIDTH)


def _to_chunks(t, chunk):
    b, n, h, d = t.shape
    return t.reshape(b, n // chunk, chunk, h, d).transpose(1, 0, 3, 2, 4)


def _hgrn_scan(q, k, v, log_f, s0):
    qc, kc, vc, gc = (_to_chunks(t.astype(jnp.float32), HG_CHUNK) for t in (q, k, v, log_f))
    tri = jnp.tril(jnp.ones((HG_CHUNK, HG_CHUNK), dtype=bool))[:, :, None]

    def step(state, blk):
        qb, kb, vb, gb = blk
        cum = jnp.cumsum(gb, axis=2)
        o_inter = jnp.einsum('bhtk,bhkv->bhtv', qb * jnp.exp(cum), state)
        diff = cum[:, :, :, None, :] - cum[:, :, None, :, :]
        decay = jnp.exp(jnp.where(tri, diff, -jnp.inf))
        scores = jnp.einsum('bhtk,bhsk,bhtsk->bhts', qb, kb, decay)
        o_intra = jnp.einsum('bhts,bhsv->bhtv', scores, vb)
        last = cum[:, :, -1:, :]
        new_state = (jnp.exp(last[:, :, 0, :])[..., None] * state
                     + jnp.einsum('bhsk,bhsv->bhkv', kb * jnp.exp(last - cum), vb))
        return new_state, o_inter + o_intra

    final, out = lax.scan(step, s0, (qc, kc, vc, gc))
    b, n, h = q.shape[:3]
    out = out.transpose(1, 0, 3, 2, 4).reshape(b, n, h, v.shape[-1])
    return out.astype(v.dtype), final


def _hgrn_direction(q, f_logit, lower_bound, v, s0):
    f = lower_bound + (1.0 - lower_bound) * jax.nn.sigmoid(f_logit.astype(jnp.float32))
    return _hgrn_scan(q, 1.0 - f, v, jnp.log(f), s0)


def _hgrn_bidir(hq, hff, hfb, hi, lb_f, lb_b, s0_f, s0_b):
    q = jax.nn.silu(_heads(hq, HG_HEADS)) * (HG_K ** -0.5)
    v = _heads(hi, HG_HEADS)
    o_f, st_f = _hgrn_direction(q, _heads(hff, HG_HEADS), lb_f, v, s0_f)
    o_b, st_b = _hgrn_direction(jnp.flip(q, 1), jnp.flip(_heads(hfb, HG_HEADS), 1), lb_b, jnp.flip(v, 1), s0_b)
    return o_f + jnp.flip(o_b, 1), st_f, st_b


def _hgrn_out(o, g, gain):
    y = _rms(o) * gain * jax.nn.silu(_heads(g, HG_HEADS))
    return y.reshape(y.shape[:-2] + (HG_WIDTH,))


def _chunk_gmlp(u, v, w_s, b_s, norm_gain):
    b, n, _ = u.shape
    nc = n // GM_CHUNK
    u = jax.nn.gelu(u, approximate=False)
    v = _rms(_heads(jax.nn.gelu(v, approximate=False), GM_GROUPS)) * norm_gain.reshape(GM_GROUPS, GM_DIM)
    vc = v.reshape(b, nc, GM_CHUNK, GM_GROUPS, GM_DIM)
    mixed = jnp.einsum('gts,bnsgc->bntgc', w_s, vc) + b_s.T[:, :, None]
    return u * mixed.reshape(b, n, GM_WIDTH)


def _swiglu(h, w_gate, w_up, w_down):
    return (jax.nn.silu(h @ w_gate) * (h @ w_up)) @ w_down


def _moe(h, router, w_gate, w_up, w_down):
    shape = h.shape
    t = h.reshape(-1, shape[-1])
    logits = (t @ router).astype(jnp.float32)
    top_val, top_idx = lax.top_k(logits, TOP_K)
    top_w = jax.nn.softmax(top_val, axis=-1)
    gate = jnp.sum(jax.nn.one_hot(top_idx, N_EXPERTS, dtype=jnp.float32) * top_w[..., None], axis=1).astype(t.dtype)
    out = jnp.zeros_like(t)
    for e in range(N_EXPERTS):
        out = out + gate[:, e:e + 1] * _swiglu(t, w_gate[e], w_up[e], w_down[e])
    return out.reshape(shape)


def _channel_mixer(h, layer, ffn_w_gate, ffn_w_up, ffn_w_down, moe_router, moe_w_gate, moe_w_up, moe_w_down):
    i = layer // 2
    if layer % 2 == 0:
        return _swiglu(h, ffn_w_gate[i], ffn_w_up[i], ffn_w_down[i])
    return _moe(h, moe_router[i], moe_w_gate[i], moe_w_up[i], moe_w_down[i])


def setup_inputs(seed: int = 0) -> dict:
    key = jax.random.key(seed)
    ks = jax.random.split(key, 24)
    D = D_MODEL

    def nrm(k, shape, s):
        return jax.random.normal(k, shape, jnp.float32) * s

    return {
        'x': nrm(ks[0], (BATCH, SEQ, D), 1.0),
        'c': nrm(ks[1], (BATCH, D), 1.0),
        'ctx': nrm(ks[2], (BATCH, CTX_LEN, D), 1.0),
        'c_ctx': nrm(ks[3], (D,), 1.0),
        'w_ada': nrm(ks[4], (DEPTH, D, 6 * D), 0.5 * D ** -0.5),
        'b_ada': nrm(ks[5], (DEPTH, 6 * D), 0.02),
        'w_in': nrm(ks[6], (DEPTH, D, IN_WIDTH), D ** -0.5),
        'w_out': nrm(ks[7], (DEPTH, MIX_WIDTH, D), MIX_WIDTH ** -0.5),
        'q_norm_gain': 1.0 + nrm(ks[8], (DEPTH, HEAD_DIM), 0.02),
        'k_norm_gain': 1.0 + nrm(ks[9], (DEPTH, HEAD_DIM), 0.02),
        'attn_sink': nrm(ks[10], (DEPTH, N_Q_HEADS), 0.5),
        'hgrn_lower_bound': nrm(ks[11], (2, DEPTH, HG_WIDTH), 1.0),
        'hgrn_out_gain': 1.0 + nrm(ks[12], (DEPTH, HG_V), 0.02),
        'gmlp_w_s': nrm(ks[13], (DEPTH, GM_GROUPS, GM_CHUNK, GM_CHUNK), GM_CHUNK ** -0.5),
        'gmlp_b_s': 1.0 + nrm(ks[14], (DEPTH, GM_GROUPS, GM_CHUNK), 0.02),
        'gmlp_norm_gain': 1.0 + nrm(ks[15], (DEPTH, GM_WIDTH), 0.02),
        'ffn_w_gate': nrm(ks[16], (N_DENSE, D, FFN_DIM), D ** -0.5),
        'ffn_w_up': nrm(ks[17], (N_DENSE, D, FFN_DIM), D ** -0.5),
        'ffn_w_down': nrm(ks[18], (N_DENSE, FFN_DIM, D), FFN_DIM ** -0.5),
        'moe_router': nrm(ks[19], (N_MOE, D, N_EXPERTS), D ** -0.5),
        'moe_w_gate': nrm(ks[20], (N_MOE, N_EXPERTS, D, EXPERT_DIM), D ** -0.5),
        'moe_w_up': nrm(ks[21], (N_MOE, N_EXPERTS, D, EXPERT_DIM), D ** -0.5),
        'moe_w_down': nrm(ks[22], (N_MOE, N_EXPERTS, EXPERT_DIM, D), EXPERT_DIM ** -0.5),
    }


def reference(x, c, ctx, c_ctx, w_ada, b_ada, w_in, w_out, q_norm_gain, k_norm_gain, attn_sink,
              hgrn_lower_bound, hgrn_out_gain, gmlp_w_s, gmlp_b_s, gmlp_norm_gain,
              ffn_w_gate, ffn_w_up, ffn_w_down, moe_router, moe_w_gate, moe_w_up, moe_w_down):
    bsz, n_lat = x.shape[:2]
    rows = n_lat // GRID_W
    rope = _axial_rope_tables(rows, x.dtype)
    lb_soft = jax.nn.softmax(hgrn_lower_bound.astype(jnp.float32), axis=1)
    lower_bounds = jnp.cumsum(lb_soft, axis=1) - lb_soft[:, :1]
    zero_state = jnp.zeros((bsz, HG_HEADS, HG_K, HG_V), jnp.float32)

    for layer in range(DEPTH):
        last = layer == DEPTH - 1
        mod = jax.nn.silu(c) @ w_ada[layer] + b_ada[layer]
        mod_c = jax.nn.silu(c_ctx) @ w_ada[layer] + b_ada[layer]
        sh_a, sc_a, g_a, sh_f, sc_f, g_f = jnp.split(mod[:, None, :], 6, axis=-1)
        csh_a, csc_a, cg_a, csh_f, csc_f, cg_f = jnp.split(mod_c, 6)

        h = _rms(x) * (1.0 + sc_a) + sh_a
        hc = _rms(ctx) * (1.0 + csc_a) + csh_a
        aq, ak, av, hq, hff, hfb, hi, hg, gu, gv = _split_cols(h @ w_in[layer])
        caq, cak, cav, chq, chff, chfb, chi, chg, cgu, cgv = _split_cols(hc @ w_in[layer])

        qg, kg = q_norm_gain[layer], k_norm_gain[layer]
        q = _apply_axial_rope(_rms(_heads(aq, N_Q_HEADS)) * qg, rope)
        k = _apply_axial_rope(_rms(_heads(ak, N_KV_HEADS)) * kg, rope)
        v = _heads(av, N_KV_HEADS)
        k_c = _rms(_heads(cak, N_KV_HEADS)) * kg
        v_c = _heads(cav, N_KV_HEADS)
        attn_lat = _window_attention(q, k, v, k_c, v_c, attn_sink[layer])

        lb_f = lower_bounds[0, layer].reshape(HG_HEADS, HG_K)
        lb_b = lower_bounds[1, layer].reshape(HG_HEADS, HG_K)
        o_hc, st_f, st_b = _hgrn_bidir(chq, chff, chfb, chi, lb_f, lb_b, zero_state, zero_state)
        o_hl, _, _ = _hgrn_bidir(hq, hff, hfb, hi, lb_f, lb_b, st_f, st_b)
        hg_lat = _hgrn_out(o_hl, hg, hgrn_out_gain[layer])

        gm_lat = _chunk_gmlp(gu, gv, gmlp_w_s[layer], gmlp_b_s[layer], gmlp_norm_gain[layer])

        mix = jnp.concatenate([attn_lat, hg_lat, gm_lat], axis=-1)
        x = x + g_a * (mix @ w_out[layer])
        if not last:
            q_c = _rms(_heads(caq, N_Q_HEADS)) * qg
            attn_ctx = _context_attention(q_c, k_c, v_c, attn_sink[layer])
            hg_ctx = _hgrn_out(o_hc, chg, hgrn_out_gain[layer])
            gm_ctx = _chunk_gmlp(cgu, cgv, gmlp_w_s[layer], gmlp_b_s[layer], gmlp_norm_gain[layer])
            mix_c = jnp.concatenate([attn_ctx, hg_ctx, gm_ctx], axis=-1)
            ctx = ctx + cg_a * (mix_c @ w_out[layer])

        h = _rms(x) * (1.0 + sc_f) + sh_f
        x = x + g_f * _channel_mixer(h, layer, ffn_w_gate, ffn_w_up, ffn_w_down,
                                     moe_router, moe_w_gate, moe_w_up, moe_w_down)
        if not last:
            hc = _rms(ctx) * (1.0 + csc_f) + csh_f
            ctx = ctx + cg_f * _channel_mixer(hc, layer, ffn_w_gate, ffn_w_up, ffn_w_down,
                                              moe_router, moe_w_gate, moe_w_up, moe_w_down)
    return x
```

```python
import functools

import numpy as np
import jax
import jax.numpy as jnp
from jax import lax
from jax.experimental import pallas as pl
from jax.experimental.pallas import tpu as pltpu

F32 = jnp.float32
BF16 = jnp.bfloat16

D = 1024
SEQ = 16384
CTX = 256
ROWS = CTX + SEQ
DEPTH = 2
GRID_W = 64
EPS = 1e-6
HEAD_DIM = 64
N_Q_HEADS = 8
N_KV_HEADS = 2
ATTN_W = 512
KV_W = 128
WINDOW = 128
ROPE_THETA = 10000.0
HG_HEADS = 4
HG_K = 64
HG_W = 256
GM_GROUPS = 4
GM_DIM = 64
GM_W = 256
GM_CHUNK = 128
IN_W = 2560
FFN_DIM = 2816
N_EXPERTS = 8
EXPERT_DIM = 3584

VMEM_LIMIT = 56 * 1024 * 1024
NEG = -1e30

TM = 256
HT = 128
HG_LEVELS = (8, 16, 32, 64)
AB = 128
FFN_TM = 640
FFN_FC = 256
MOE_TM = 1024
MOE_FC = 512


def _params(sem, vmem=VMEM_LIMIT):
    return pltpu.CompilerParams(dimension_semantics=sem, vmem_limit_bytes=vmem)


def _block_ones(n, seg):
    i = np.arange(n)
    return (i[:, None] // seg == i[None, :] // seg).astype(np.float32)


def _ada_kernel(a_ref, w_ref, b_ref, o_ref):
    a = a_ref[...]
    s = (a * jax.nn.sigmoid(a)).astype(BF16)
    w = w_ref[0].astype(BF16)
    o_ref[0] = jnp.dot(s, w, preferred_element_type=F32) + b_ref[0]


def _ada(cc, w_ada, b_ada):
    tn = 1536
    return pl.pallas_call(
        _ada_kernel,
        grid=(DEPTH, 6 * D // tn),
        in_specs=[
            pl.BlockSpec((8, D), lambda l, n: (0, 0)),
            pl.BlockSpec((1, D, tn), lambda l, n: (l, 0, n)),
            pl.BlockSpec((1, 1, tn), lambda l, n: (l, 0, n)),
        ],
        out_specs=pl.BlockSpec((1, 8, tn), lambda l, n: (l, 0, n)),
        out_shape=jax.ShapeDtypeStruct((DEPTH, 8, 6 * D), F32),
        compiler_params=_params(("parallel", "parallel")),
        name="ada_mod",
    )(cc, w_ada, b_ada.reshape(DEPTH, 1, 6 * D))


def _gelu(x):
    return 0.5 * x * (1.0 + lax.erf(x * 0.7071067811865476))


def _inproj_kernel(x_ref, mod_ref, w_ref, cs_ref, sn_ref, gain_ref, bq_ref, bg_ref, gng_ref, ws_ref, bs_ref,
                   q_ref, k_ref, v_ref, ph_ref, gm_ref):
    x = x_ref[...]
    r = lax.rsqrt(jnp.mean(x * x, axis=-1, keepdims=True) + EPS)
    mod = mod_ref[0]
    h = (x * r) * (1.0 + mod[1:2]) + mod[0:1]
    p = jnp.dot(h.astype(BF16), w_ref[...], preferred_element_type=F32)

    qk = p[:, :ATTN_W + KV_W]
    ss = jnp.dot((qk * qk).astype(BF16), bq_ref[...], preferred_element_type=F32)
    qkn = qk * lax.rsqrt(ss * (1.0 / HEAD_DIM) + EPS) * gain_ref[...]
    lane = lax.broadcasted_iota(jnp.int32, (TM, 128), 1)
    first_half = (lane // 16) % 2 == 0
    cs = cs_ref[...]
    sn = sn_ref[...]
    rot = []
    for j in range(5):
        xs = qkn[:, 128 * j:128 * (j + 1)]
        partner = jnp.where(first_half, pltpu.roll(xs, 112, 1), pltpu.roll(xs, 16, 1))
        rot.append(xs * cs + partner * sn)
    q_ref[...] = jnp.concatenate(rot[:4], axis=1).astype(BF16)

    low = lane < 64

    def dup(t):
        sw = pltpu.roll(t, 64, 1)
        return jnp.concatenate([jnp.where(low, t, sw), jnp.where(low, sw, t)], axis=1)

    k_ref[...] = dup(rot[4]).astype(BF16)
    v_ref[...] = dup(p[:, ATTN_W + KV_W:ATTN_W + 2 * KV_W]).astype(BF16)

    ph_ref[...] = p[:, 768:2048]

    u = _gelu(p[:, 2048:2304])
    vv = _gelu(p[:, 2304:2560])
    ssv = jnp.dot((vv * vv).astype(BF16), bg_ref[...], preferred_element_type=F32)
    vn = vv * lax.rsqrt(ssv * (1.0 / GM_DIM) + EPS) * gng_ref[...]
    group = lax.broadcasted_iota(jnp.int32, (GM_CHUNK, GM_W), 1) // GM_DIM
    for c in range(TM // GM_CHUNK):
        vc = vn[GM_CHUNK * c:GM_CHUNK * (c + 1)]
        acc = bs_ref[...]
        for g in range(GM_GROUPS):
            vg = jnp.where(group == g, vc, 0.0).astype(BF16)
            acc = acc + jnp.dot(ws_ref[g], vg, preferred_element_type=F32)
        gm_ref[GM_CHUNK * c:GM_CHUNK * (c + 1), :] = (u[GM_CHUNK * c:GM_CHUNK * (c + 1)] * acc).astype(BF16)


def _inproj(xs, modsel, w_in, cs, sn, gain, bq, bg, gng, ws, bs):
    nt = ROWS // TM
    const2 = lambda i: (0, 0)
    return pl.pallas_call(
        _inproj_kernel,
        grid=(nt,),
        in_specs=[
            pl.BlockSpec((TM, D), lambda i: (i, 0)),
            pl.BlockSpec((1, 8, D), lambda i: (jnp.minimum(i, 1), 0, 0)),
            pl.BlockSpec((D, IN_W), const2),
            pl.BlockSpec((TM, 128), lambda i: (i, 0)),
            pl.BlockSpec((TM, 128), lambda i: (i, 0)),
            pl.BlockSpec((1, ATTN_W + KV_W), const2),
            pl.BlockSpec((ATTN_W + KV_W, ATTN_W + KV_W), const2),
            pl.BlockSpec((GM_W, GM_W), const2),
            pl.BlockSpec((1, GM_W), const2),
            pl.BlockSpec((GM_GROUPS, GM_CHUNK, GM_CHUNK), lambda i: (0, 0, 0)),
            pl.BlockSpec((GM_CHUNK, GM_W), const2),
        ],
        out_specs=[
            pl.BlockSpec((TM, ATTN_W), lambda i: (i, 0)),
            pl.BlockSpec((TM, 2 * KV_W), lambda i: (i, 0)),
            pl.BlockSpec((TM, 2 * KV_W), lambda i: (i, 0)),
            pl.BlockSpec((TM, 5 * HG_W), lambda i: (i, 0)),
            pl.BlockSpec((TM, GM_W), lambda i: (i, 0)),
        ],
        out_shape=[
            jax.ShapeDtypeStruct((ROWS, ATTN_W), BF16),
            jax.ShapeDtypeStruct((ROWS, 2 * KV_W), BF16),
            jax.ShapeDtypeStruct((ROWS, 2 * KV_W), BF16),
            jax.ShapeDtypeStruct((ROWS, 5 * HG_W), F32),
            jax.ShapeDtypeStruct((ROWS, GM_W), BF16),
        ],
        compiler_params=_params(("parallel",)),
        name="inproj",
    )(xs, modsel, w_in, cs, sn, gain, bq, bg, gng, ws, bs)


def _hgrn_consts():
    t = np.arange(HT)
    tt, ss = t[:, None], t[None, :]
    stack, masks = [], []
    for m in HG_LEVELS:
        same = (tt // m) == (ss // m)
        stack += [same & (ss <= tt), same & (ss > tt)]
        masks.append(((tt // (2 * m)) == (ss // (2 * m))) & ((tt % (2 * m)) >= m) & ((ss % (2 * m)) < m))
    stack.append(ss <= tt)
    masks.append(((tt // 8) == (ss // 8)) & (ss <= tt))
    lst_f = np.concatenate(stack, axis=0).astype(np.float32)
    lst_b = np.concatenate([b.T for b in stack], axis=0).astype(np.float32)
    msk_f = np.stack(masks).astype(np.float32)
    msk_b = np.stack([m.T for m in masks]).astype(np.float32)
    e = np.zeros((8, HG_W, 128), np.float32)
    tl = np.zeros((HG_HEADS, 128, HT), np.float32)
    for h in range(HG_HEADS):
        for j in range(8):
            e[j, h * HG_K:(h + 1) * HG_K, h * 8 + j] = 1.0
            tl[h, h * 8 + j, (t % 8) == j] = 1.0
    return lst_f, lst_b, msk_f, msk_b, e, tl, _block_ones(HG_W, HG_K)


def _hgrn_direction(zq, zf, vi, lb, lst_ref, msk_ref, e_ref, tl_ref, bd_ref, st_ref, q3_ref, k3_ref, p3_ref):
    q = (zq * jax.nn.sigmoid(zq)) * (HG_K ** -0.5)
    f = lb + (1.0 - lb) * jax.nn.sigmoid(zf)
    g = jnp.log(f)
    kk = 1.0 - f
    g_hi = g.astype(BF16)
    g_lo = (g - g_hi.astype(F32)).astype(BF16)
    lst = lst_ref[...]
    cum = jnp.dot(lst, g_hi, preferred_element_type=F32) + jnp.dot(lst, g_lo, preferred_element_type=F32)
    nl = len(HG_LEVELS)
    c = cum[2 * nl * HT:(2 * nl + 1) * HT]
    ctot = jnp.sum(g, axis=0, keepdims=True)
    head = lax.broadcasted_iota(jnp.int32, (HT, HG_W), 1) // HG_K
    vb = vi.astype(BF16)
    nt_dims = (((1,), (1,)), ((), ()))

    st = st_ref[...]
    qe = (q * jnp.exp(c)).astype(BF16)
    o = lax.dot_general(qe, st.astype(BF16), nt_dims, preferred_element_type=F32)
    ke = (kk * jnp.exp(ctot - c)).astype(BF16)
    upd = lax.dot_general(vb, ke, (((0,), (0,)), ((), ())), preferred_element_type=F32)
    st_ref[...] = st * jnp.exp(ctot) + upd * bd_ref[...]

    p8 = cum[0:HT]
    q3_ref[...] = q.reshape(HT // 8, 8, HG_W)
    k3_ref[...] = kk.reshape(HT // 8, 8, HG_W)
    p3_ref[...] = p8.reshape(HT // 8, 8, HG_W)
    a = jnp.zeros((HT, 128), F32)
    for j in range(8):
        kj = k3_ref[:, j:j + 1, :]
        pj = p3_ref[:, j:j + 1, :]
        slab = q3_ref[...] * kj * jnp.exp(jnp.minimum(p3_ref[...] - pj, 0.0))
        a = a + jnp.dot(slab.reshape(HT, HG_W).astype(BF16), e_ref[j], preferred_element_type=F32)
    ab = a.astype(BF16)
    sc = [msk_ref[nl] * jnp.dot(ab, tl_ref[h], preferred_element_type=F32) for h in range(HG_HEADS)]

    for li in range(nl):
        pm = cum[2 * li * HT:(2 * li + 1) * HT]
        rm = cum[(2 * li + 1) * HT:(2 * li + 2) * HT]
        qt = q * jnp.exp(pm)
        kt = (kk * jnp.exp(rm)).astype(BF16)
        for h in range(HG_HEADS):
            qh = jnp.where(head == h, qt, 0.0).astype(BF16)
            sc[h] = sc[h] + msk_ref[li] * lax.dot_general(qh, kt, nt_dims, preferred_element_type=F32)

    for h in range(HG_HEADS):
        vh = jnp.where(head == h, vi, 0.0).astype(BF16)
        o = o + jnp.dot(sc[h].astype(BF16), vh, preferred_element_type=F32)
    return o


def _hgrn_kernel(qf_ref, ff_ref, if_ref, qb_ref, fb_ref, ib_ref, lbf_ref, lbb_ref,
                 lstf_ref, lstb_ref, mskf_ref, mskb_ref, e_ref, tl_ref, bd_ref,
                 of_ref, ob_ref, stf_ref, stb_ref, q3f, k3f, p3f, q3b, k3b, p3b):
    @pl.when(pl.program_id(0) == 0)
    def _():
        stf_ref[...] = jnp.zeros_like(stf_ref)
        stb_ref[...] = jnp.zeros_like(stb_ref)

    of_ref[...] = _hgrn_direction(qf_ref[...], ff_ref[...], if_ref[...], lbf_ref[...], lstf_ref, mskf_ref,
                                  e_ref, tl_ref, bd_ref, stf_ref, q3f, k3f, p3f)
    ob_ref[...] = _hgrn_direction(qb_ref[...], fb_ref[...], ib_ref[...], lbb_ref[...], lstb_ref, mskb_ref,
                                  e_ref, tl_ref, bd_ref, stb_ref, q3b, k3b, p3b)


def _hgrn(ph, lbf, lbb, consts):
    lst_f, lst_b, msk_f, msk_b, e, tl, bd = consts
    nt = ROWS // HT
    nc = CTX // HT

    def fwd(col):
        return pl.BlockSpec((HT, HG_W), lambda j: (j, col))

    def bwd(col):
        return pl.BlockSpec((HT, HG_W), lambda j: (jnp.where(j < nc, nc - 1 - j, nt + nc - 1 - j), col))

    def const(a):
        nd = a.ndim
        return pl.BlockSpec(a.shape, lambda j: (0,) * nd)

    scr3 = pltpu.VMEM((HT // 8, 8, HG_W), F32)
    return pl.pallas_call(
        _hgrn_kernel,
        grid=(nt,),
        in_specs=[fwd(0), fwd(1), fwd(3), bwd(0), bwd(2), bwd(3), const(lbf), const(lbb),
                  const(lst_f), const(lst_b), const(msk_f), const(msk_b), const(e), const(tl), const(bd)],
        out_specs=[pl.BlockSpec((HT, HG_W), lambda j: (j, 0)),
                   pl.BlockSpec((HT, HG_W), lambda j: (jnp.where(j < nc, nc - 1 - j, nt + nc - 1 - j), 0))],
        out_shape=[jax.ShapeDtypeStruct((ROWS, HG_W), F32), jax.ShapeDtypeStruct((ROWS, HG_W), F32)],
        scratch_shapes=[pltpu.VMEM((HG_W, HG_W), F32), pltpu.VMEM((HG_W, HG_W), F32),
                        scr3, scr3, scr3, scr3, scr3, scr3],
        compiler_params=_params(("arbitrary",)),
        name="hgrn_scan",
    )(ph, ph, ph, ph, ph, ph, lbf, lbb, lst_f, lst_b, msk_f, msk_b, e, tl, bd)


def _attn_core(q, kcat, vcat, bias, sink_ref, hm_ref, o_ref):
    nk = kcat.shape[0]
    nt_dims = (((1,), (1,)), ((), ()))
    pairs = []
    for g in range(N_KV_HEADS):
        kg = kcat[:, 128 * g:128 * (g + 1)]
        vg = vcat[:, 128 * g:128 * (g + 1)]
        vsel = (vg * hm_ref[0, :nk, :], vg * hm_ref[1, :nk, :])
        qs = []
        for pr in (2 * g, 2 * g + 1):
            qp = q[:, 128 * pr:128 * (pr + 1)]
            qs += [qp * hm_ref[0, :AB, :], qp * hm_ref[1, :AB, :]]
        s = lax.dot_general(jnp.concatenate(qs, axis=0), kg, nt_dims, preferred_element_type=F32)
        for pr in range(2):
            acc = None
            for a in range(2):
                hh = 2 * pr + a
                sk = sink_ref[4 * g + hh]
                sh = s[AB * hh:AB * (hh + 1)]
                if bias is not None:
                    sh = sh + bias
                m = jnp.maximum(jnp.max(sh, axis=1, keepdims=True), sk)
                e = jnp.exp(sh - m)
                den = jnp.sum(e, axis=1, keepdims=True) + jnp.exp(sk - m)
                pv = jnp.dot(e.astype(BF16), vsel[a], preferred_element_type=F32) * (1.0 / den)
                acc = pv if acc is None else acc + pv
            pairs.append(acc)
    o_ref[...] = jnp.concatenate(pairs, axis=1).astype(BF16)


def _attn_lat_kernel(q_ref, km_ref, k0_ref, kp_ref, kc_ref, vm_ref, v0_ref, vp_ref, vc_ref, sink_ref, hm_ref, o_ref):
    i = pl.program_id(0)
    nb = pl.num_programs(0)
    kcat = jnp.concatenate([km_ref[...], k0_ref[...], kp_ref[...], kc_ref[...]], axis=0)
    vcat = jnp.concatenate([vm_ref[...], v0_ref[...], vp_ref[...], vc_ref[...]], axis=0)
    nk = 3 * AB + CTX
    a = lax.broadcasted_iota(jnp.int32, (AB, nk), 0)
    r = lax.broadcasted_iota(jnp.int32, (AB, nk), 1)
    lo = jnp.where(i == 0, AB, 0)
    hi = jnp.where(i == nb - 1, 2 * AB - 1, 3 * AB - 1)
    lower = jnp.maximum(a + (AB - WINDOW), lo)
    upper = jnp.minimum(a + (AB + WINDOW), hi)
    in_band = jnp.where(r >= lower, jnp.where(r <= upper, 0.0, NEG), NEG)
    bias = jnp.where(r >= 3 * AB, 0.0, in_band)
    _attn_core(q_ref[...], kcat, vcat, bias, sink_ref, hm_ref, o_ref)


def _attn_ctx_kernel(q_ref, kc_ref, vc_ref, sink_ref, hm_ref, o_ref):
    _attn_core(q_ref[...], kc_ref[...], vc_ref[...], None, sink_ref, hm_ref, o_ref)


def _head_masks():
    hm = np.zeros((2, 3 * AB + CTX, 128), np.float32)
    hm[0, :, :64] = 1.0
    hm[1, :, 64:] = 1.0
    return hm


def _attn_lat(q, k, v, sink, hm):
    nb = SEQ // AB
    off = CTX // AB
    last = ROWS // AB - 1
    kv = lambda f: pl.BlockSpec((AB, 2 * KV_W), f)
    prev = lambda i: (i + off - 1, 0)
    cur = lambda i: (i + off, 0)
    nxt = lambda i: (jnp.minimum(i + off + 1, last), 0)
    ctx = pl.BlockSpec((CTX, 2 * KV_W), lambda i: (0, 0))
    return pl.pallas_call(
        _attn_lat_kernel,
        grid=(nb,),
        in_specs=[pl.BlockSpec((AB, ATTN_W), cur), kv(prev), kv(cur), kv(nxt), ctx,
                  kv(prev), kv(cur), kv(nxt), ctx,
                  pl.BlockSpec(memory_space=pltpu.SMEM),
                  pl.BlockSpec((2, 3 * AB + CTX, 128), lambda i: (0, 0, 0))],
        out_specs=pl.BlockSpec((AB, ATTN_W), lambda i: (i, 0)),
        out_shape=jax.ShapeDtypeStruct((SEQ, ATTN_W), BF16),
        compiler_params=_params(("parallel",)),
        name="attn_window",
    )(q, k, k, k, k, v, v, v, v, sink, hm)


def _attn_ctx(q, k, v, sink, hm):
    ctx = pl.BlockSpec((CTX, 2 * KV_W), lambda i: (0, 0))
    return pl.pallas_call(
        _attn_ctx_kernel,
        grid=(CTX // AB,),
        in_specs=[pl.BlockSpec((AB, ATTN_W), lambda i: (i, 0)), ctx, ctx,
                  pl.BlockSpec(memory_space=pltpu.SMEM),
                  pl.BlockSpec((2, 3 * AB + CTX, 128), lambda i: (0, 0, 0))],
        out_specs=pl.BlockSpec((AB, ATTN_W), lambda i: (i, 0)),
        out_shape=jax.ShapeDtypeStruct((CTX, ATTN_W), BF16),
        compiler_params=_params(("parallel",)),
        name="attn_context",
    )(q, k, v, sink, hm)


def _outproj_kernel(*refs, with_ctx, with_router):
    it = iter(refs)
    x_ref = next(it)
    mod_ref = next(it)
    al_ref = next(it)
    ac_ref = next(it) if with_ctx else None
    of_ref, ob_ref, hg_ref, gm_ref, w_ref, hgain_ref, bh_ref = (next(it) for _ in range(7))
    rt_ref = next(it) if with_router else None
    x1_ref = next(it)
    h2_ref = next(it)
    gate_ref = next(it) if with_router else None

    mod = mod_ref[0]
    attn = al_ref[...]
    if with_ctx:
        attn = jnp.where(pl.program_id(0) == 0, ac_ref[...], attn)
    o = of_ref[...] + ob_ref[...]
    ss = jnp.dot((o * o).astype(BF16), bh_ref[...], preferred_element_type=F32)
    hg = hg_ref[...]
    y = o * lax.rsqrt(ss * (1.0 / HG_K) + EPS) * hgain_ref[...] * (hg * jax.nn.sigmoid(hg))
    mix = jnp.concatenate([attn, y.astype(BF16), gm_ref[...]], axis=1)
    proj = jnp.dot(mix, w_ref[...], preferred_element_type=F32)
    x1 = x_ref[...] + mod[2:3] * proj
    x1_ref[...] = x1
    r = lax.rsqrt(jnp.mean(x1 * x1, axis=-1, keepdims=True) + EPS)
    h2 = (x1 * r) * (1.0 + mod[4:5]) + mod[3:4]
    h2_ref[...] = h2.astype(BF16)

    if with_router:
        logits = jnp.dot(h2, rt_ref[...], preferred_element_type=F32, precision=lax.Precision.HIGHEST)
        lane = lax.broadcasted_iota(jnp.int32, logits.shape, 1)
        l1 = jnp.where(lane < N_EXPERTS, logits, NEG)
        m1 = jnp.max(l1, axis=1, keepdims=True)
        i1 = jnp.min(jnp.where(l1 == m1, lane, 128), axis=1, keepdims=True)
        l2 = jnp.where(lane == i1, NEG, l1)
        m2 = jnp.max(l2, axis=1, keepdims=True)
        i2 = jnp.min(jnp.where(l2 == m2, lane, 128), axis=1, keepdims=True)
        e2 = jnp.exp(m2 - m1)
        w1 = 1.0 / (1.0 + e2)
        gate_ref[...] = jnp.where(lane == i1, w1, 0.0) + jnp.where(lane == i2, e2 * w1, 0.0)


def _outproj(xs, modsel, attn_lat, attn_ctx, o_f, o_b, ph, gm, w_out, hgain, bh, router, *, with_ctx):
    with_router = router is not None
    off = 0 if with_ctx else CTX // TM
    nt = ROWS // TM - off
    rows_out = ROWS if with_ctx else SEQ
    row = lambda w: pl.BlockSpec((TM, w), lambda i: (i + off, 0))
    const2 = lambda i: (0, 0)
    in_specs = [row(D),
                pl.BlockSpec((1, 8, D), lambda i: (jnp.minimum(i + off, 1), 0, 0)),
                pl.BlockSpec((TM, ATTN_W), lambda i: (jnp.maximum(i + off - CTX // TM, 0), 0))]
    args = [xs, modsel, attn_lat]
    if with_ctx:
        in_specs.append(pl.BlockSpec((TM, ATTN_W), const2))
        args.append(attn_ctx)
    in_specs += [row(HG_W), row(HG_W),
                 pl.BlockSpec((TM, HG_W), lambda i: (i + off, 4)),
                 row(GM_W),
                 pl.BlockSpec((D, D), const2),
                 pl.BlockSpec((1, HG_W), const2),
                 pl.BlockSpec((HG_W, HG_W), const2)]
    args += [o_f, o_b, ph, gm, w_out, hgain, bh]
    out_specs = [pl.BlockSpec((TM, D), lambda i: (i, 0)), pl.BlockSpec((TM, D), lambda i: (i, 0))]
    out_shape = [jax.ShapeDtypeStruct((rows_out, D), F32), jax.ShapeDtypeStruct((rows_out, D), BF16)]
    if with_router:
        in_specs.append(pl.BlockSpec((D, 128), const2))
        args.append(router)
        out_specs.append(pl.BlockSpec((TM, 128), lambda i: (i, 0)))
        out_shape.append(jax.ShapeDtypeStruct((rows_out, 128), F32))
    return pl.pallas_call(
        functools.partial(_outproj_kernel, with_ctx=with_ctx, with_router=with_router),
        grid=(nt,),
        in_specs=in_specs,
        out_specs=out_specs,
        out_shape=out_shape,
        compiler_params=_params(("parallel",)),
        name="outproj",
    )(*args)


def _ffn_kernel(h_ref, x_ref, mod_ref, wg_ref, wu_ref, wd_ref, o_ref):
    h = h_ref[...]

    def body(f, acc):
        a = jnp.dot(h, wg_ref[f], preferred_element_type=F32)
        b = jnp.dot(h, wu_ref[f], preferred_element_type=F32)
        t = (a * jax.nn.sigmoid(a) * b).astype(BF16)
        return acc + jnp.dot(t, wd_ref[f], preferred_element_type=F32)

    acc = lax.fori_loop(0, FFN_DIM // FFN_FC, body, jnp.zeros((FFN_TM, D), F32))
    row = pl.program_id(0) * FFN_TM + lax.broadcasted_iota(jnp.int32, (FFN_TM, D), 0)
    gate = jnp.where(row < CTX, mod_ref[0][5:6], mod_ref[1][5:6])
    o_ref[...] = x_ref[...] + gate * acc


def _ffn(h2, x1, modsel, wg, wu, wd):
    nt = ROWS // FFN_TM
    nf = FFN_DIM // FFN_FC
    return pl.pallas_call(
        _ffn_kernel,
        grid=(nt,),
        in_specs=[pl.BlockSpec((FFN_TM, D), lambda i: (i, 0)),
                  pl.BlockSpec((FFN_TM, D), lambda i: (i, 0)),
                  pl.BlockSpec((2, 8, D), lambda i: (0, 0, 0)),
                  pl.BlockSpec((nf, D, FFN_FC), lambda i: (0, 0, 0)),
                  pl.BlockSpec((nf, D, FFN_FC), lambda i: (0, 0, 0)),
                  pl.BlockSpec((nf, FFN_FC, D), lambda i: (0, 0, 0))],
        out_specs=pl.BlockSpec((FFN_TM, D), lambda i: (i, 0)),
        out_shape=jax.ShapeDtypeStruct((ROWS, D), F32),
        compiler_params=_params(("parallel",)),
        name="ffn_dense",
    )(h2, x1, modsel, wg, wu, wd)


def _moe_kernel(h_ref, gate_ref, x_ref, gf_ref, wg_ref, wu_ref, wd_ref, o_ref, acc_ref):
    e = pl.program_id(1)
    f = pl.program_id(2)

    @pl.when((e == 0) & (f == 0))
    def _():
        acc_ref[...] = jnp.zeros_like(acc_ref)

    h = h_ref[...]
    a = jnp.dot(h, wg_ref[0, 0].astype(BF16), preferred_element_type=F32)
    b = jnp.dot(h, wu_ref[0, 0].astype(BF16), preferred_element_type=F32)
    gate = gate_ref[...]
    lane = lax.broadcasted_iota(jnp.int32, gate.shape, 1)
    ge = jnp.sum(jnp.where(lane == e, gate, 0.0), axis=1, keepdims=True)
    t = (a * jax.nn.sigmoid(a) * b * ge).astype(BF16)
    acc_ref[...] += jnp.dot(t, wd_ref[0, 0].astype(BF16), preferred_element_type=F32)

    @pl.when((e == pl.num_programs(1) - 1) & (f == pl.num_programs(2) - 1))
    def _():
        o_ref[...] = x_ref[...] + gf_ref[...] * acc_ref[...]


def _moe(h2, gate, x1, gf, wg, wu, wd):
    nt = SEQ // MOE_TM
    nf = EXPERT_DIM // MOE_FC
    row = lambda w: pl.BlockSpec((MOE_TM, w), lambda i, e, f: (i, 0))
    return pl.pallas_call(
        _moe_kernel,
        grid=(nt, N_EXPERTS, nf),
        in_specs=[row(D), row(128), row(D),
                  pl.BlockSpec((1, D), lambda i, e, f: (0, 0)),
                  pl.BlockSpec((1, 1, D, MOE_FC), lambda i, e, f: (0, e, 0, f)),
                  pl.BlockSpec((1, 1, D, MOE_FC), lambda i, e, f: (0, e, 0, f)),
                  pl.BlockSpec((1, 1, MOE_FC, D), lambda i, e, f: (0, e, f, 0))],
        out_specs=row(D),
        out_shape=jax.ShapeDtypeStruct((SEQ, D), F32),
        scratch_shapes=[pltpu.VMEM((MOE_TM, D), F32)],
        compiler_params=_params(("parallel", "arbitrary", "arbitrary")),
        name="moe_dense",
    )(h2, gate, x1, gf, wg, wu, wd)


def _rope_tables():
    rows = SEQ // GRID_W
    row = jnp.repeat(jnp.arange(rows, dtype=F32), GRID_W, total_repeat_length=SEQ)
    col = jnp.tile(jnp.arange(GRID_W, dtype=F32), rows)
    axis_dim = HEAD_DIM // 2
    inv_freq = ROPE_THETA ** (-jnp.arange(0, axis_dim, 2, dtype=F32) / axis_dim)
    ang_r = row[:, None] * inv_freq
    ang_c = col[:, None] * inv_freq
    cos_r, sin_r, cos_c, sin_c = jnp.cos(ang_r), jnp.sin(ang_r), jnp.cos(ang_c), jnp.sin(ang_c)
    cs = jnp.concatenate([cos_r, cos_r, cos_c, cos_c], axis=1)
    sn = jnp.concatenate([-sin_r, sin_r, -sin_c, sin_c], axis=1)
    cs = jnp.concatenate([jnp.ones((CTX, HEAD_DIM), F32), cs], axis=0)
    sn = jnp.concatenate([jnp.zeros((CTX, HEAD_DIM), F32), sn], axis=0)
    return jnp.tile(cs, (1, 2)), jnp.tile(sn, (1, 2))


def kernel(x, c, ctx, c_ctx, w_ada, b_ada, w_in, w_out, q_norm_gain, k_norm_gain, attn_sink, hgrn_lower_bound, hgrn_out_gain, gmlp_w_s, gmlp_b_s, gmlp_norm_gain, ffn_w_gate, ffn_w_up, ffn_w_down, moe_router, moe_w_gate, moe_w_up, moe_w_down):
    cc = jnp.zeros((8, D), F32).at[0].set(c[0]).at[1].set(c_ctx)
    mod = _ada(cc, w_ada, b_ada)
    mod6 = mod[:, :2].reshape(DEPTH, 2, 6, D)
    modsel = jnp.zeros((DEPTH, 2, 8, D), F32).at[:, 0, :6].set(mod6[:, 1]).at[:, 1, :6].set(mod6[:, 0])

    lb_soft = jax.nn.softmax(hgrn_lower_bound.astype(F32), axis=1)
    lower = jnp.cumsum(lb_soft, axis=1) - lb_soft[:, :1]

    cs, sn = _rope_tables()
    hconsts = _hgrn_consts()
    hconsts = tuple(jnp.asarray(a, BF16) for a in hconsts[:2]) + tuple(jnp.asarray(a, F32) for a in hconsts[2:4]) \
        + tuple(jnp.asarray(a, BF16) for a in hconsts[4:6]) + (jnp.asarray(hconsts[6], F32),)
    bq = jnp.asarray(_block_ones(ATTN_W + KV_W, HEAD_DIM), BF16)
    bg = jnp.asarray(_block_ones(GM_W, GM_DIM), BF16)
    hm = jnp.asarray(_head_masks(), BF16)

    xs = jnp.concatenate([ctx[0], x[0]], axis=0)
    for layer in range(DEPTH):
        last = layer == DEPTH - 1
        gain = jnp.concatenate([jnp.tile(q_norm_gain[layer], N_Q_HEADS) * (HEAD_DIM ** -0.5),
                                jnp.tile(k_norm_gain[layer], N_KV_HEADS)])[None, :]
        bias = jnp.repeat(gmlp_b_s[layer].T, GM_DIM, axis=1)
        q, k, v, ph, gm = _inproj(xs, modsel[layer], w_in[layer].astype(BF16), cs, sn, gain, bq, bg,
                                  gmlp_norm_gain[layer][None, :], gmlp_w_s[layer].astype(BF16), bias)
        o_f, o_b = _hgrn(ph, lower[0, layer][None, :], lower[1, layer][None, :], hconsts)
        sink = attn_sink[layer]
        attn_lat = _attn_lat(q, k, v, sink, hm)
        hgain = jnp.tile(hgrn_out_gain[layer], HG_HEADS)[None, :]
        w_o = w_out[layer].astype(BF16)
        if not last:
            attn_ctx = _attn_ctx(q, k, v, sink, hm)
            x1, h2 = _outproj(xs, modsel[layer], attn_lat, attn_ctx, o_f, o_b, ph, gm, w_o, hgain, bg, None,
                              with_ctx=True)
        else:
            router = jnp.zeros((D, 128), F32).at[:, :N_EXPERTS].set(moe_router[layer // 2])
            x1, h2, gate = _outproj(xs, modsel[layer], attn_lat, None, o_f, o_b, ph, gm, w_o, hgain, bg, router,
                                    with_ctx=False)
        i = layer // 2
        if layer % 2 == 0:
            nf = FFN_DIM // FFN_FC
            wg = ffn_w_gate[i].astype(BF16).reshape(D, nf, FFN_FC).transpose(1, 0, 2)
            wu = ffn_w_up[i].astype(BF16).reshape(D, nf, FFN_FC).transpose(1, 0, 2)
            wd = ffn_w_down[i].astype(BF16).reshape(nf, FFN_FC, D)
            xs = _ffn(h2, x1, modsel[layer], wg, wu, wd)
        else:
            xs = _moe(h2, gate, x1, modsel[layer, 1, 5][None, :], moe_w_gate[i:i + 1], moe_w_up[i:i + 1],
                      moe_w_down[i:i + 1])
    return xs[None]
```

```python
import functools

import numpy as np
import jax
import jax.numpy as jnp
from jax import lax
from jax.experimental import pallas as pl
from jax.experimental.pallas import tpu as pltpu

F32 = jnp.float32
BF16 = jnp.bfloat16

D = 1024
SEQ = 16384
CTX = 256
ROWS = CTX + SEQ
DEPTH = 2
GRID_W = 64
EPS = 1e-6
HEAD_DIM = 64
N_Q_HEADS = 8
N_KV_HEADS = 2
ATTN_W = 512
KV_W = 128
WINDOW = 128
ROPE_THETA = 10000.0
HG_HEADS = 4
HG_K = 64
HG_W = 256
GM_GROUPS = 4
GM_DIM = 64
GM_W = 256
GM_CHUNK = 128
IN_W = 2560
FFN_DIM = 2816
N_EXPERTS = 8
EXPERT_DIM = 3584

VMEM_LIMIT = 56 * 1024 * 1024
NEG = -1e30

TM = 256
HT = 128
HG_LEVELS = (8, 16, 32, 64)
AB = 128
FFN_TM = 640
FFN_FC = 256
MOE_TILE = 256
MOE_CAP = 17 * MOE_TILE
MOE_ITEMS = 2 * SEQ // MOE_CAP + N_EXPERTS
MOE_GC = 512
MOE_NG = SEQ // MOE_GC
MOE_FC = 512
MOE_NF = EXPERT_DIM // MOE_FC
MOE_CW = 256
MOE_CT = 128
MOE_CS = MOE_CW // MOE_CT + 1
MOE_VMEM_LIMIT = 60 * 1024 * 1024


def _params(sem, vmem=VMEM_LIMIT):
    return pltpu.CompilerParams(dimension_semantics=sem, vmem_limit_bytes=vmem)


def _block_ones(n, seg):
    i = np.arange(n)
    return (i[:, None] // seg == i[None, :] // seg).astype(np.float32)


def _ada_kernel(a_ref, w_ref, b_ref, o_ref):
    a = a_ref[...]
    s = (a * jax.nn.sigmoid(a)).astype(BF16)
    w = w_ref[0].astype(BF16)
    o_ref[0] = jnp.dot(s, w, preferred_element_type=F32) + b_ref[0]


def _ada(cc, w_ada, b_ada):
    tn = 1536
    return pl.pallas_call(
        _ada_kernel,
        grid=(DEPTH, 6 * D // tn),
        in_specs=[
            pl.BlockSpec((8, D), lambda l, n: (0, 0)),
            pl.BlockSpec((1, D, tn), lambda l, n: (l, 0, n)),
            pl.BlockSpec((1, 1, tn), lambda l, n: (l, 0, n)),
        ],
        out_specs=pl.BlockSpec((1, 8, tn), lambda l, n: (l, 0, n)),
        out_shape=jax.ShapeDtypeStruct((DEPTH, 8, 6 * D), F32),
        compiler_params=_params(("parallel", "parallel")),
        name="ada_mod",
    )(cc, w_ada, b_ada.reshape(DEPTH, 1, 6 * D))


def _gelu(x):
    return 0.5 * x * (1.0 + lax.erf(x * 0.7071067811865476))


def _inproj_kernel(x_ref, mod_ref, w_ref, cs_ref, sn_ref, gain_ref, bq_ref, bg_ref, gng_ref, ws_ref, bs_ref,
                   q_ref, k_ref, v_ref, ph_ref, gm_ref):
    x = x_ref[...]
    r = lax.rsqrt(jnp.mean(x * x, axis=-1, keepdims=True) + EPS)
    mod = mod_ref[0]
    h = (x * r) * (1.0 + mod[1:2]) + mod[0:1]
    p = jnp.dot(h.astype(BF16), w_ref[...], preferred_element_type=F32)

    qk = p[:, :ATTN_W + KV_W]
    ss = jnp.dot((qk * qk).astype(BF16), bq_ref[...], preferred_element_type=F32)
    qkn = qk * lax.rsqrt(ss * (1.0 / HEAD_DIM) + EPS) * gain_ref[...]
    lane = lax.broadcasted_iota(jnp.int32, (TM, 128), 1)
    first_half = (lane // 16) % 2 == 0
    cs = cs_ref[...]
    sn = sn_ref[...]
    rot = []
    for j in range(5):
        xs = qkn[:, 128 * j:128 * (j + 1)]
        partner = jnp.where(first_half, pltpu.roll(xs, 112, 1), pltpu.roll(xs, 16, 1))
        rot.append(xs * cs + partner * sn)
    q_ref[...] = jnp.concatenate(rot[:4], axis=1).astype(BF16)

    low = lane < 64

    def dup(t):
        sw = pltpu.roll(t, 64, 1)
        return jnp.concatenate([jnp.where(low, t, sw), jnp.where(low, sw, t)], axis=1)

    k_ref[...] = dup(rot[4]).astype(BF16)
    v_ref[...] = dup(p[:, ATTN_W + KV_W:ATTN_W + 2 * KV_W]).astype(BF16)

    ph_ref[...] = p[:, 768:2048]

    u = _gelu(p[:, 2048:2304])
    vv = _gelu(p[:, 2304:2560])
    ssv = jnp.dot((vv * vv).astype(BF16), bg_ref[...], preferred_element_type=F32)
    vn = vv * lax.rsqrt(ssv * (1.0 / GM_DIM) + EPS) * gng_ref[...]
    group = lax.broadcasted_iota(jnp.int32, (GM_CHUNK, GM_W), 1) // GM_DIM
    for c in range(TM // GM_CHUNK):
        vc = vn[GM_CHUNK * c:GM_CHUNK * (c + 1)]
        acc = bs_ref[...]
        for g in range(GM_GROUPS):
            vg = jnp.where(group == g, vc, 0.0).astype(BF16)
            acc = acc + jnp.dot(ws_ref[g], vg, preferred_element_type=F32)
        gm_ref[GM_CHUNK * c:GM_CHUNK * (c + 1), :] = (u[GM_CHUNK * c:GM_CHUNK * (c + 1)] * acc).astype(BF16)


def _inproj(xs, modsel, w_in, cs, sn, gain, bq, bg, gng, ws, bs):
    nt = ROWS // TM
    const2 = lambda i: (0, 0)
    return pl.pallas_call(
        _inproj_kernel,
        grid=(nt,),
        in_specs=[
            pl.BlockSpec((TM, D), lambda i: (i, 0)),
            pl.BlockSpec((1, 8, D), lambda i: (jnp.minimum(i, 1), 0, 0)),
            pl.BlockSpec((D, IN_W), const2),
            pl.BlockSpec((TM, 128), lambda i: (i, 0)),
            pl.BlockSpec((TM, 128), lambda i: (i, 0)),
            pl.BlockSpec((1, ATTN_W + KV_W), const2),
            pl.BlockSpec((ATTN_W + KV_W, ATTN_W + KV_W), const2),
            pl.BlockSpec((GM_W, GM_W), const2),
            pl.BlockSpec((1, GM_W), const2),
            pl.BlockSpec((GM_GROUPS, GM_CHUNK, GM_CHUNK), lambda i: (0, 0, 0)),
            pl.BlockSpec((GM_CHUNK, GM_W), const2),
        ],
        out_specs=[
            pl.BlockSpec((TM, ATTN_W), lambda i: (i, 0)),
            pl.BlockSpec((TM, 2 * KV_W), lambda i: (i, 0)),
            pl.BlockSpec((TM, 2 * KV_W), lambda i: (i, 0)),
            pl.BlockSpec((TM, 5 * HG_W), lambda i: (i, 0)),
            pl.BlockSpec((TM, GM_W), lambda i: (i, 0)),
        ],
        out_shape=[
            jax.ShapeDtypeStruct((ROWS, ATTN_W), BF16),
            jax.ShapeDtypeStruct((ROWS, 2 * KV_W), BF16),
            jax.ShapeDtypeStruct((ROWS, 2 * KV_W), BF16),
            jax.ShapeDtypeStruct((ROWS, 5 * HG_W), F32),
            jax.ShapeDtypeStruct((ROWS, GM_W), BF16),
        ],
        compiler_params=_params(("parallel",)),
        name="inproj",
    )(xs, modsel, w_in, cs, sn, gain, bq, bg, gng, ws, bs)


def _hgrn_consts():
    t = np.arange(HT)
    tt, ss = t[:, None], t[None, :]
    stack, masks = [], []
    for m in HG_LEVELS:
        same = (tt // m) == (ss // m)
        stack += [same & (ss <= tt), same & (ss > tt)]
        masks.append(((tt // (2 * m)) == (ss // (2 * m))) & ((tt % (2 * m)) >= m) & ((ss % (2 * m)) < m))
    stack.append(ss <= tt)
    masks.append(((tt // 8) == (ss // 8)) & (ss <= tt))
    lst_f = np.concatenate(stack, axis=0).astype(np.float32)
    lst_b = np.concatenate([b.T for b in stack], axis=0).astype(np.float32)
    msk_f = np.stack(masks).astype(np.float32)
    msk_b = np.stack([m.T for m in masks]).astype(np.float32)
    e = np.zeros((8, HG_W, 128), np.float32)
    tl = np.zeros((HG_HEADS, 128, HT), np.float32)
    for h in range(HG_HEADS):
        for j in range(8):
            e[j, h * HG_K:(h + 1) * HG_K, h * 8 + j] = 1.0
            tl[h, h * 8 + j, (t % 8) == j] = 1.0
    return lst_f, lst_b, msk_f, msk_b, e, tl, _block_ones(HG_W, HG_K)


def _hgrn_direction(zq, zf, vi, lb, lst_ref, msk_ref, e_ref, tl_ref, bd_ref, st_ref, q3_ref, k3_ref, p3_ref):
    q = (zq * jax.nn.sigmoid(zq)) * (HG_K ** -0.5)
    f = lb + (1.0 - lb) * jax.nn.sigmoid(zf)
    g = jnp.log(f)
    kk = 1.0 - f
    g_hi = g.astype(BF16)
    g_lo = (g - g_hi.astype(F32)).astype(BF16)
    lst = lst_ref[...]
    cum = jnp.dot(lst, g_hi, preferred_element_type=F32) + jnp.dot(lst, g_lo, preferred_element_type=F32)
    nl = len(HG_LEVELS)
    c = cum[2 * nl * HT:(2 * nl + 1) * HT]
    ctot = jnp.sum(g, axis=0, keepdims=True)
    head = lax.broadcasted_iota(jnp.int32, (HT, HG_W), 1) // HG_K
    vb = vi.astype(BF16)
    nt_dims = (((1,), (1,)), ((), ()))

    st = st_ref[...]
    qe = (q * jnp.exp(c)).astype(BF16)
    o = lax.dot_general(qe, st.astype(BF16), nt_dims, preferred_element_type=F32)
    ke = (kk * jnp.exp(ctot - c)).astype(BF16)
    upd = lax.dot_general(vb, ke, (((0,), (0,)), ((), ())), preferred_element_type=F32)
    st_ref[...] = st * jnp.exp(ctot) + upd * bd_ref[...]

    p8 = cum[0:HT]
    q3_ref[...] = q.reshape(HT // 8, 8, HG_W)
    k3_ref[...] = kk.reshape(HT // 8, 8, HG_W)
    p3_ref[...] = p8.reshape(HT // 8, 8, HG_W)
    a = jnp.zeros((HT, 128), F32)
    for j in range(8):
        kj = k3_ref[:, j:j + 1, :]
        pj = p3_ref[:, j:j + 1, :]
        slab = q3_ref[...] * kj * jnp.exp(jnp.minimum(p3_ref[...] - pj, 0.0))
        a = a + jnp.dot(slab.reshape(HT, HG_W).astype(BF16), e_ref[j], preferred_element_type=F32)
    ab = a.astype(BF16)
    sc = [msk_ref[nl] * jnp.dot(ab, tl_ref[h], preferred_element_type=F32) for h in range(HG_HEADS)]

    for li in range(nl):
        pm = cum[2 * li * HT:(2 * li + 1) * HT]
        rm = cum[(2 * li + 1) * HT:(2 * li + 2) * HT]
        qt = q * jnp.exp(pm)
        kt = (kk * jnp.exp(rm)).astype(BF16)
        for h in range(HG_HEADS):
            qh = jnp.where(head == h, qt, 0.0).astype(BF16)
            sc[h] = sc[h] + msk_ref[li] * lax.dot_general(qh, kt, nt_dims, preferred_element_type=F32)

    for h in range(HG_HEADS):
        vh = jnp.where(head == h, vi, 0.0).astype(BF16)
        o = o + jnp.dot(sc[h].astype(BF16), vh, preferred_element_type=F32)
    return o


def _hgrn_kernel(qf_ref, ff_ref, if_ref, qb_ref, fb_ref, ib_ref, lbf_ref, lbb_ref,
                 lstf_ref, lstb_ref, mskf_ref, mskb_ref, e_ref, tl_ref, bd_ref,
                 of_ref, ob_ref, stf_ref, stb_ref, q3f, k3f, p3f, q3b, k3b, p3b):
    @pl.when(pl.program_id(0) == 0)
    def _():
        stf_ref[...] = jnp.zeros_like(stf_ref)
        stb_ref[...] = jnp.zeros_like(stb_ref)

    of_ref[...] = _hgrn_direction(qf_ref[...], ff_ref[...], if_ref[...], lbf_ref[...], lstf_ref, mskf_ref,
                                  e_ref, tl_ref, bd_ref, stf_ref, q3f, k3f, p3f)
    ob_ref[...] = _hgrn_direction(qb_ref[...], fb_ref[...], ib_ref[...], lbb_ref[...], lstb_ref, mskb_ref,
                                  e_ref, tl_ref, bd_ref, stb_ref, q3b, k3b, p3b)


def _hgrn(ph, lbf, lbb, consts):
    lst_f, lst_b, msk_f, msk_b, e, tl, bd = consts
    nt = ROWS // HT
    nc = CTX // HT

    def fwd(col):
        return pl.BlockSpec((HT, HG_W), lambda j: (j, col))

    def bwd(col):
        return pl.BlockSpec((HT, HG_W), lambda j: (jnp.where(j < nc, nc - 1 - j, nt + nc - 1 - j), col))

    def const(a):
        nd = a.ndim
        return pl.BlockSpec(a.shape, lambda j: (0,) * nd)

    scr3 = pltpu.VMEM((HT // 8, 8, HG_W), F32)
    return pl.pallas_call(
        _hgrn_kernel,
        grid=(nt,),
        in_specs=[fwd(0), fwd(1), fwd(3), bwd(0), bwd(2), bwd(3), const(lbf), const(lbb),
                  const(lst_f), const(lst_b), const(msk_f), const(msk_b), const(e), const(tl), const(bd)],
        out_specs=[pl.BlockSpec((HT, HG_W), lambda j: (j, 0)),
                   pl.BlockSpec((HT, HG_W), lambda j: (jnp.where(j < nc, nc - 1 - j, nt + nc - 1 - j), 0))],
        out_shape=[jax.ShapeDtypeStruct((ROWS, HG_W), F32), jax.ShapeDtypeStruct((ROWS, HG_W), F32)],
        scratch_shapes=[pltpu.VMEM((HG_W, HG_W), F32), pltpu.VMEM((HG_W, HG_W), F32),
                        scr3, scr3, scr3, scr3, scr3, scr3],
        compiler_params=_params(("arbitrary",)),
        name="hgrn_scan",
    )(ph, ph, ph, ph, ph, ph, lbf, lbb, lst_f, lst_b, msk_f, msk_b, e, tl, bd)


def _attn_core(q, kcat, vcat, bias, sink_ref, hm_ref, o_ref):
    nk = kcat.shape[0]
    nt_dims = (((1,), (1,)), ((), ()))
    pairs = []
    for g in range(N_KV_HEADS):
        kg = kcat[:, 128 * g:128 * (g + 1)]
        vg = vcat[:, 128 * g:128 * (g + 1)]
        vsel = (vg * hm_ref[0, :nk, :], vg * hm_ref[1, :nk, :])
        qs = []
        for pr in (2 * g, 2 * g + 1):
            qp = q[:, 128 * pr:128 * (pr + 1)]
            qs += [qp * hm_ref[0, :AB, :], qp * hm_ref[1, :AB, :]]
        s = lax.dot_general(jnp.concatenate(qs, axis=0), kg, nt_dims, preferred_element_type=F32)
        for pr in range(2):
            acc = None
            for a in range(2):
                hh = 2 * pr + a
                sk = sink_ref[4 * g + hh]
                sh = s[AB * hh:AB * (hh + 1)]
                if bias is not None:
                    sh = sh + bias
                m = jnp.maximum(jnp.max(sh, axis=1, keepdims=True), sk)
                e = jnp.exp(sh - m)
                den = jnp.sum(e, axis=1, keepdims=True) + jnp.exp(sk - m)
                pv = jnp.dot(e.astype(BF16), vsel[a], preferred_element_type=F32) * (1.0 / den)
                acc = pv if acc is None else acc + pv
            pairs.append(acc)
    o_ref[...] = jnp.concatenate(pairs, axis=1).astype(BF16)


def _attn_lat_kernel(q_ref, km_ref, k0_ref, kp_ref, kc_ref, vm_ref, v0_ref, vp_ref, vc_ref, sink_ref, hm_ref, o_ref):
    i = pl.program_id(0)
    nb = pl.num_programs(0)
    kcat = jnp.concatenate([km_ref[...], k0_ref[...], kp_ref[...], kc_ref[...]], axis=0)
    vcat = jnp.concatenate([vm_ref[...], v0_ref[...], vp_ref[...], vc_ref[...]], axis=0)
    nk = 3 * AB + CTX
    a = lax.broadcasted_iota(jnp.int32, (AB, nk), 0)
    r = lax.broadcasted_iota(jnp.int32, (AB, nk), 1)
    lo = jnp.where(i == 0, AB, 0)
    hi = jnp.where(i == nb - 1, 2 * AB - 1, 3 * AB - 1)
    lower = jnp.maximum(a + (AB - WINDOW), lo)
    upper = jnp.minimum(a + (AB + WINDOW), hi)
    in_band = jnp.where(r >= lower, jnp.where(r <= upper, 0.0, NEG), NEG)
    bias = jnp.where(r >= 3 * AB, 0.0, in_band)
    _attn_core(q_ref[...], kcat, vcat, bias, sink_ref, hm_ref, o_ref)


def _attn_ctx_kernel(q_ref, kc_ref, vc_ref, sink_ref, hm_ref, o_ref):
    _attn_core(q_ref[...], kc_ref[...], vc_ref[...], None, sink_ref, hm_ref, o_ref)


def _head_masks():
    hm = np.zeros((2, 3 * AB + CTX, 128), np.float32)
    hm[0, :, :64] = 1.0
    hm[1, :, 64:] = 1.0
    return hm


def _attn_lat(q, k, v, sink, hm):
    nb = SEQ // AB
    off = CTX // AB
    last = ROWS // AB - 1
    kv = lambda f: pl.BlockSpec((AB, 2 * KV_W), f)
    prev = lambda i: (i + off - 1, 0)
    cur = lambda i: (i + off, 0)
    nxt = lambda i: (jnp.minimum(i + off + 1, last), 0)
    ctx = pl.BlockSpec((CTX, 2 * KV_W), lambda i: (0, 0))
    return pl.pallas_call(
        _attn_lat_kernel,
        grid=(nb,),
        in_specs=[pl.BlockSpec((AB, ATTN_W), cur), kv(prev), kv(cur), kv(nxt), ctx,
                  kv(prev), kv(cur), kv(nxt), ctx,
                  pl.BlockSpec(memory_space=pltpu.SMEM),
                  pl.BlockSpec((2, 3 * AB + CTX, 128), lambda i: (0, 0, 0))],
        out_specs=pl.BlockSpec((AB, ATTN_W), lambda i: (i, 0)),
        out_shape=jax.ShapeDtypeStruct((SEQ, ATTN_W), BF16),
        compiler_params=_params(("parallel",)),
        name="attn_window",
    )(q, k, k, k, k, v, v, v, v, sink, hm)


def _attn_ctx(q, k, v, sink, hm):
    ctx = pl.BlockSpec((CTX, 2 * KV_W), lambda i: (0, 0))
    return pl.pallas_call(
        _attn_ctx_kernel,
        grid=(CTX // AB,),
        in_specs=[pl.BlockSpec((AB, ATTN_W), lambda i: (i, 0)), ctx, ctx,
                  pl.BlockSpec(memory_space=pltpu.SMEM),
                  pl.BlockSpec((2, 3 * AB + CTX, 128), lambda i: (0, 0, 0))],
        out_specs=pl.BlockSpec((AB, ATTN_W), lambda i: (i, 0)),
        out_shape=jax.ShapeDtypeStruct((CTX, ATTN_W), BF16),
        compiler_params=_params(("parallel",)),
        name="attn_context",
    )(q, k, v, sink, hm)


def _outproj_kernel(*refs, with_ctx, with_router):
    it = iter(refs)
    x_ref = next(it)
    mod_ref = next(it)
    al_ref = next(it)
    ac_ref = next(it) if with_ctx else None
    of_ref, ob_ref, hg_ref, gm_ref, w_ref, hgain_ref, bh_ref = (next(it) for _ in range(7))
    rt_ref = next(it) if with_router else None
    lt_ref = next(it) if with_router else None
    x1_ref = next(it)
    h2_ref = next(it)
    if with_router:
        gate_ref, rank_ref, cnt_ref, base_ref = (next(it) for _ in range(4))

    mod = mod_ref[0]
    attn = al_ref[...]
    if with_ctx:
        attn = jnp.where(pl.program_id(0) == 0, ac_ref[...], attn)
    o = of_ref[...] + ob_ref[...]
    ss = jnp.dot((o * o).astype(BF16), bh_ref[...], preferred_element_type=F32)
    hg = hg_ref[...]
    y = o * lax.rsqrt(ss * (1.0 / HG_K) + EPS) * hgain_ref[...] * (hg * jax.nn.sigmoid(hg))
    mix = jnp.concatenate([attn, y.astype(BF16), gm_ref[...]], axis=1)
    proj = jnp.dot(mix, w_ref[...], preferred_element_type=F32)
    x1 = x_ref[...] + mod[2:3] * proj
    x1_ref[...] = x1
    r = lax.rsqrt(jnp.mean(x1 * x1, axis=-1, keepdims=True) + EPS)
    h2 = (x1 * r) * (1.0 + mod[4:5]) + mod[3:4]
    h2_ref[...] = h2.astype(BF16)

    if with_router:
        logits = jnp.dot(h2, rt_ref[...], preferred_element_type=F32, precision=lax.Precision.HIGHEST)
        lane = lax.broadcasted_iota(jnp.int32, logits.shape, 1)
        l1 = jnp.where(lane < N_EXPERTS, logits, NEG)
        m1 = jnp.max(l1, axis=1, keepdims=True)
        i1 = jnp.min(jnp.where(l1 == m1, lane, 128), axis=1, keepdims=True)
        l2 = jnp.where(lane == i1, NEG, l1)
        m2 = jnp.max(l2, axis=1, keepdims=True)
        i2 = jnp.min(jnp.where(l2 == m2, lane, 128), axis=1, keepdims=True)
        e2 = jnp.exp(m2 - m1)
        w1 = 1.0 / (1.0 + e2)
        gate_ref[...] = jnp.where(lane == i1, w1, 0.0) + jnp.where(lane == i2, e2 * w1, 0.0)

        @pl.when(pl.program_id(0) == 0)
        def _():
            base_ref[...] = jnp.zeros_like(base_ref)

        mem = jnp.where(lane == i1, 1.0, jnp.where(lane == i2, 1.0, 0.0))
        local = jnp.dot(lt_ref[...], mem.astype(BF16), preferred_element_type=F32)
        base = base_ref[...]
        rank_ref[...] = jnp.where(mem > 0.0, base + local, -1.0)
        after = base + jnp.sum(mem, axis=0, keepdims=True)
        cnt_ref[0, 0:1, :] = base
        cnt_ref[0, 1:2, :] = after
        cnt_ref[0, 2:8, :] = jnp.zeros((6, 128), F32)
        base_ref[...] = after


def _outproj(xs, modsel, attn_lat, attn_ctx, o_f, o_b, ph, gm, w_out, hgain, bh, router, *, with_ctx):
    with_router = router is not None
    off = 0 if with_ctx else CTX // TM
    nt = ROWS // TM - off
    rows_out = ROWS if with_ctx else SEQ
    row = lambda w: pl.BlockSpec((TM, w), lambda i: (i + off, 0))
    const2 = lambda i: (0, 0)
    in_specs = [row(D),
                pl.BlockSpec((1, 8, D), lambda i: (jnp.minimum(i + off, 1), 0, 0)),
                pl.BlockSpec((TM, ATTN_W), lambda i: (jnp.maximum(i + off - CTX // TM, 0), 0))]
    args = [xs, modsel, attn_lat]
    if with_ctx:
        in_specs.append(pl.BlockSpec((TM, ATTN_W), const2))
        args.append(attn_ctx)
    in_specs += [row(HG_W), row(HG_W),
                 pl.BlockSpec((TM, HG_W), lambda i: (i + off, 4)),
                 row(GM_W),
                 pl.BlockSpec((D, D), const2),
                 pl.BlockSpec((1, HG_W), const2),
                 pl.BlockSpec((HG_W, HG_W), const2)]
    args += [o_f, o_b, ph, gm, w_out, hgain, bh]
    out_specs = [pl.BlockSpec((TM, D), lambda i: (i, 0)), pl.BlockSpec((TM, D), lambda i: (i, 0))]
    out_shape = [jax.ShapeDtypeStruct((rows_out, D), F32), jax.ShapeDtypeStruct((rows_out, D), BF16)]
    scratch = []
    if with_router:
        t = np.arange(TM)
        strict_lower = jnp.asarray((t[None, :] < t[:, None]).astype(np.float32), BF16)
        in_specs += [pl.BlockSpec((D, 128), const2), pl.BlockSpec((TM, TM), const2)]
        args += [router, strict_lower]
        out_specs += [pl.BlockSpec((TM, 128), lambda i: (i, 0)), pl.BlockSpec((TM, 128), lambda i: (i, 0)),
                      pl.BlockSpec((1, 8, 128), lambda i: (i, 0, 0))]
        out_shape += [jax.ShapeDtypeStruct((rows_out, 128), F32), jax.ShapeDtypeStruct((rows_out, 128), F32),
                      jax.ShapeDtypeStruct((nt, 8, 128), F32)]
        scratch = [pltpu.VMEM((1, 128), F32)]
    return pl.pallas_call(
        functools.partial(_outproj_kernel, with_ctx=with_ctx, with_router=with_router),
        grid=(nt,),
        in_specs=in_specs,
        out_specs=out_specs,
        out_shape=out_shape,
        scratch_shapes=scratch,
        compiler_params=_params(("arbitrary",) if with_router else ("parallel",)),
        name="outproj",
    )(*args)


def _ffn_kernel(h_ref, x_ref, mod_ref, wg_ref, wu_ref, wd_ref, o_ref):
    h = h_ref[...]

    def body(f, acc):
        a = jnp.dot(h, wg_ref[f], preferred_element_type=F32)
        b = jnp.dot(h, wu_ref[f], preferred_element_type=F32)
        t = (a * jax.nn.sigmoid(a) * b).astype(BF16)
        return acc + jnp.dot(t, wd_ref[f], preferred_element_type=F32)

    acc = lax.fori_loop(0, FFN_DIM // FFN_FC, body, jnp.zeros((FFN_TM, D), F32))
    row = pl.program_id(0) * FFN_TM + lax.broadcasted_iota(jnp.int32, (FFN_TM, D), 0)
    gate = jnp.where(row < CTX, mod_ref[0][5:6], mod_ref[1][5:6])
    o_ref[...] = x_ref[...] + gate * acc


def _ffn(h2, x1, modsel, wg, wu, wd):
    nt = ROWS // FFN_TM
    nf = FFN_DIM // FFN_FC
    return pl.pallas_call(
        _ffn_kernel,
        grid=(nt,),
        in_specs=[pl.BlockSpec((FFN_TM, D), lambda i: (i, 0)),
                  pl.BlockSpec((FFN_TM, D), lambda i: (i, 0)),
                  pl.BlockSpec((2, 8, D), lambda i: (0, 0, 0)),
                  pl.BlockSpec((nf, D, FFN_FC), lambda i: (0, 0, 0)),
                  pl.BlockSpec((nf, D, FFN_FC), lambda i: (0, 0, 0)),
                  pl.BlockSpec((nf, FFN_FC, D), lambda i: (0, 0, 0))],
        out_specs=pl.BlockSpec((FFN_TM, D), lambda i: (i, 0)),
        out_shape=jax.ShapeDtypeStruct((ROWS, D), F32),
        compiler_params=_params(("parallel",)),
        name="ffn_dense",
    )(h2, x1, modsel, wg, wu, wd)


def _moe_index(i, s, ie, ibase, irows, cnt):
    valid = irows[i] > 0
    chunk = jnp.where(valid, jnp.minimum(s, MOE_NG - 1), MOE_NG - 1)
    fidx = jnp.where(valid, jnp.maximum(s - MOE_NG, 0), MOE_NF - 1)
    return chunk, fidx


def _moe_expert_kernel(ie_ref, ibase_ref, irows_ref, cnt_ref,
                       h_ref, rk_ref, wg_ref, wu_ref, wd_ref, ys_ref, yg_ref):
    i = pl.program_id(0)
    s = pl.program_id(1)
    e = ie_ref[i]
    base = ibase_ref[i]
    rows = irows_ref[i]
    ntile = (rows + MOE_TILE - 1) // MOE_TILE

    @pl.when(s == 0)
    def _():
        ys_ref[...] = jnp.zeros_like(ys_ref)

    @pl.when((s < MOE_NG) & (rows > 0))
    def _():
        lo = jnp.maximum(cnt_ref[s * N_EXPERTS + e] - base, 0)
        hi = jnp.minimum(cnt_ref[(s + 1) * N_EXPERTS + e] - base, rows) - 1
        a_lo = lo // MOE_TILE
        a_hi = jnp.where(hi >= lo, hi // MOE_TILE, a_lo - 1)
        rel = rk_ref[pl.ds(e, 1), :].astype(jnp.int32) - base
        h = h_ref[...]

        def tile_body(a, carry):
            off = pl.multiple_of(a * MOE_TILE, MOE_TILE)
            slot = lax.broadcasted_iota(jnp.int32, (MOE_TILE, MOE_GC), 0) + off
            onehot = jnp.where(rel == slot, 1.0, 0.0).astype(BF16)
            packed = jnp.dot(onehot, h, preferred_element_type=F32)
            ys_ref[pl.ds(off, MOE_TILE), :] = (ys_ref[pl.ds(off, MOE_TILE), :].astype(F32) + packed).astype(BF16)
            return carry

        lax.fori_loop(a_lo, a_hi + 1, tile_body, 0)

    def expert_step(mode):
        wg = wg_ref[0, 0].astype(BF16)
        wu = wu_ref[0, 0].astype(BF16)
        wd = wd_ref[0, 0].astype(BF16)

        def tile_body(k, carry):
            off = pl.multiple_of(k * MOE_TILE, MOE_TILE)
            xg = ys_ref[pl.ds(off, MOE_TILE), :]
            a = jnp.dot(xg, wg, preferred_element_type=F32)
            b = jnp.dot(xg, wu, preferred_element_type=F32)
            y = jnp.dot((a * jax.nn.sigmoid(a) * b).astype(BF16), wd, preferred_element_type=F32)
            if mode == "first":
                yg_ref[pl.ds(off, MOE_TILE), :] = y
            elif mode == "middle":
                yg_ref[pl.ds(off, MOE_TILE), :] += y
            else:
                ys_ref[pl.ds(off, MOE_TILE), :] = (yg_ref[pl.ds(off, MOE_TILE), :] + y).astype(BF16)
            return carry

        lax.fori_loop(0, ntile, tile_body, 0)

    f = s - MOE_NG
    pl.when((f == 0) & (rows > 0))(functools.partial(expert_step, "first"))
    pl.when((f > 0) & (f < MOE_NF - 1) & (rows > 0))(functools.partial(expert_step, "middle"))
    pl.when((f == MOE_NF - 1) & (rows > 0))(functools.partial(expert_step, "last"))


def _moe_experts(tables, h2, rk_t, wg, wu, wd):
    ie, ibase, irows, cnt = tables

    def hmap(i, s, ie, ibase, irows, cnt):
        return (_moe_index(i, s, ie, ibase, irows, cnt)[0], 0)

    def rmap(i, s, ie, ibase, irows, cnt):
        return (0, _moe_index(i, s, ie, ibase, irows, cnt)[0])

    def upmap(i, s, ie, ibase, irows, cnt):
        return (0, ie[i], 0, _moe_index(i, s, ie, ibase, irows, cnt)[1])

    def downmap(i, s, ie, ibase, irows, cnt):
        return (0, ie[i], _moe_index(i, s, ie, ibase, irows, cnt)[1], 0)

    grid_spec = pltpu.PrefetchScalarGridSpec(
        num_scalar_prefetch=4,
        grid=(MOE_ITEMS, MOE_NG + MOE_NF),
        in_specs=[pl.BlockSpec((MOE_GC, D), hmap),
                  pl.BlockSpec((N_EXPERTS, MOE_GC), rmap),
                  pl.BlockSpec((1, 1, D, MOE_FC), upmap),
                  pl.BlockSpec((1, 1, D, MOE_FC), upmap),
                  pl.BlockSpec((1, 1, MOE_FC, D), downmap)],
        out_specs=pl.BlockSpec((MOE_CAP, D), lambda i, s, ie, ibase, irows, cnt: (i, 0)),
        scratch_shapes=[pltpu.VMEM((MOE_CAP, D), F32)],
    )
    return pl.pallas_call(
        _moe_expert_kernel,
        grid_spec=grid_spec,
        out_shape=jax.ShapeDtypeStruct((MOE_ITEMS * MOE_CAP, D), BF16),
        compiler_params=_params(("arbitrary", "arbitrary"), vmem=MOE_VMEM_LIMIT),
        name="moe_experts",
    )(ie, ibase, irows, cnt, h2, rk_t, wg, wu, wd)


def _moe_combine_kernel(tix_ref, se_ref, x_ref, gate_ref, rk_ref, gf_ref, *refs):
    y_refs, o_ref = refs[:-1], refs[-1]
    w = pl.program_id(0)
    gate = gate_ref[...]
    rk = rk_ref[...]
    lane = lax.broadcasted_iota(jnp.int32, gate.shape, 1)
    slot = lax.broadcasted_iota(jnp.int32, (MOE_CW, MOE_CS * MOE_CT), 1)
    acc = jnp.zeros((MOE_CW, D), F32)
    for e in range(N_EXPERTS):
        rcol = jnp.sum(jnp.where(lane == e, rk, 0.0), axis=1, keepdims=True)
        gcol = jnp.sum(jnp.where(lane == e, gate, 0.0), axis=1, keepdims=True)
        pos = rcol.astype(jnp.int32) + (se_ref[e] - tix_ref[w * N_EXPERTS + e] * MOE_CT)
        pos = jnp.where(rcol < 0.0, -1, pos)
        onehot = jnp.where(pos == slot, 1.0, 0.0).astype(BF16)
        ycat = jnp.concatenate([y_refs[MOE_CS * e + j][...] for j in range(MOE_CS)], axis=0)
        acc = acc + gcol * jnp.dot(onehot, ycat, preferred_element_type=F32)
    o_ref[...] = x_ref[...] + gf_ref[...] * acc


def _moe_combine(tix, se, x1, gate, rk, gf, ys):
    row = lambda wd_: pl.BlockSpec((MOE_CW, wd_), lambda w, tix, se: (w, 0))

    def ymap(e, j):
        return lambda w, tix, se: (tix[w * N_EXPERTS + e] + j, 0)

    y_specs = [pl.BlockSpec((MOE_CT, D), ymap(e, j)) for e in range(N_EXPERTS) for j in range(MOE_CS)]
    grid_spec = pltpu.PrefetchScalarGridSpec(
        num_scalar_prefetch=2,
        grid=(SEQ // MOE_CW,),
        in_specs=[row(D), row(128), row(128), pl.BlockSpec((1, D), lambda w, tix, se: (0, 0))] + y_specs,
        out_specs=row(D),
    )
    return pl.pallas_call(
        _moe_combine_kernel,
        grid_spec=grid_spec,
        out_shape=jax.ShapeDtypeStruct((SEQ, D), F32),
        compiler_params=_params(("arbitrary",)),
        name="moe_combine",
    )(tix, se, x1, gate, rk, gf, *([ys] * (N_EXPERTS * MOE_CS)))


def _moe_tables(cnt):
    before = cnt[:, 0, :N_EXPERTS].astype(jnp.int32)
    total = cnt[-1, 1, :N_EXPERTS].astype(jnp.int32)
    nit = (total + MOE_CAP - 1) // MOE_CAP
    ends = jnp.cumsum(nit)
    first = ends - nit
    item = jnp.arange(MOE_ITEMS, dtype=jnp.int32)
    n_items = ends[-1]
    ie = jnp.minimum(jnp.sum((ends[None, :] <= item[:, None]).astype(jnp.int32), axis=1), N_EXPERTS - 1)
    last_e = ie[jnp.maximum(n_items - 1, 0)]
    valid = item < n_items
    ibase = jnp.where(valid, (item - first[ie]) * MOE_CAP, 0)
    irows = jnp.where(valid, jnp.clip(total[ie] - ibase, 0, MOE_CAP), 0)
    ie = jnp.where(valid, ie, last_e)
    per_chunk = MOE_GC // TM
    cnt_g = jnp.concatenate([before[::per_chunk], total[None, :]], axis=0).reshape(-1)
    se = first * MOE_CAP
    per_win = MOE_CW // TM
    start = se[None, :] + before[::per_win]
    tix = jnp.minimum(start // MOE_CT, MOE_ITEMS * MOE_CAP // MOE_CT - MOE_CS).reshape(-1)
    return (ie, ibase.astype(jnp.int32), irows.astype(jnp.int32), cnt_g), (tix.astype(jnp.int32), se.astype(jnp.int32))


def _rope_tables():
    rows = SEQ // GRID_W
    row = jnp.repeat(jnp.arange(rows, dtype=F32), GRID_W, total_repeat_length=SEQ)
    col = jnp.tile(jnp.arange(GRID_W, dtype=F32), rows)
    axis_dim = HEAD_DIM // 2
    inv_freq = ROPE_THETA ** (-jnp.arange(0, axis_dim, 2, dtype=F32) / axis_dim)
    ang_r = row[:, None] * inv_freq
    ang_c = col[:, None] * inv_freq
    cos_r, sin_r, cos_c, sin_c = jnp.cos(ang_r), jnp.sin(ang_r), jnp.cos(ang_c), jnp.sin(ang_c)
    cs = jnp.concatenate([cos_r, cos_r, cos_c, cos_c], axis=1)
    sn = jnp.concatenate([-sin_r, sin_r, -sin_c, sin_c], axis=1)
    cs = jnp.concatenate([jnp.ones((CTX, HEAD_DIM), F32), cs], axis=0)
    sn = jnp.concatenate([jnp.zeros((CTX, HEAD_DIM), F32), sn], axis=0)
    return jnp.tile(cs, (1, 2)), jnp.tile(sn, (1, 2))


def kernel(x, c, ctx, c_ctx, w_ada, b_ada, w_in, w_out, q_norm_gain, k_norm_gain, attn_sink, hgrn_lower_bound, hgrn_out_gain, gmlp_w_s, gmlp_b_s, gmlp_norm_gain, ffn_w_gate, ffn_w_up, ffn_w_down, moe_router, moe_w_gate, moe_w_up, moe_w_down):
    cc = jnp.zeros((8, D), F32).at[0].set(c[0]).at[1].set(c_ctx)
    mod = _ada(cc, w_ada, b_ada)
    mod6 = mod[:, :2].reshape(DEPTH, 2, 6, D)
    modsel = jnp.zeros((DEPTH, 2, 8, D), F32).at[:, 0, :6].set(mod6[:, 1]).at[:, 1, :6].set(mod6[:, 0])

    lb_soft = jax.nn.softmax(hgrn_lower_bound.astype(F32), axis=1)
    lower = jnp.cumsum(lb_soft, axis=1) - lb_soft[:, :1]

    cs, sn = _rope_tables()
    hconsts = _hgrn_consts()
    hconsts = tuple(jnp.asarray(a, BF16) for a in hconsts[:2]) + tuple(jnp.asarray(a, F32) for a in hconsts[2:4]) \
        + tuple(jnp.asarray(a, BF16) for a in hconsts[4:6]) + (jnp.asarray(hconsts[6], F32),)
    bq = jnp.asarray(_block_ones(ATTN_W + KV_W, HEAD_DIM), BF16)
    bg = jnp.asarray(_block_ones(GM_W, GM_DIM), BF16)
    hm = jnp.asarray(_head_masks(), BF16)

    xs = jnp.concatenate([ctx[0], x[0]], axis=0)
    for layer in range(DEPTH):
        last = layer == DEPTH - 1
        gain = jnp.concatenate([jnp.tile(q_norm_gain[layer], N_Q_HEADS) * (HEAD_DIM ** -0.5),
                                jnp.tile(k_norm_gain[layer], N_KV_HEADS)])[None, :]
        bias = jnp.repeat(gmlp_b_s[layer].T, GM_DIM, axis=1)
        q, k, v, ph, gm = _inproj(xs, modsel[layer], w_in[layer].astype(BF16), cs, sn, gain, bq, bg,
                                  gmlp_norm_gain[layer][None, :], gmlp_w_s[layer].astype(BF16), bias)
        o_f, o_b = _hgrn(ph, lower[0, layer][None, :], lower[1, layer][None, :], hconsts)
        sink = attn_sink[layer]
        attn_lat = _attn_lat(q, k, v, sink, hm)
        hgain = jnp.tile(hgrn_out_gain[layer], HG_HEADS)[None, :]
        w_o = w_out[layer].astype(BF16)
        if not last:
            attn_ctx = _attn_ctx(q, k, v, sink, hm)
            x1, h2 = _outproj(xs, modsel[layer], attn_lat, attn_ctx, o_f, o_b, ph, gm, w_o, hgain, bg, None,
                              with_ctx=True)
        else:
            router = jnp.zeros((D, 128), F32).at[:, :N_EXPERTS].set(moe_router[layer // 2])
            x1, h2, gate, rk, cnt = _outproj(xs, modsel[layer], attn_lat, None, o_f, o_b, ph, gm, w_o, hgain, bg,
                                             router, with_ctx=False)
        i = layer // 2
        if layer % 2 == 0:
            nf = FFN_DIM // FFN_FC
            wg = ffn_w_gate[i].astype(BF16).reshape(D, nf, FFN_FC).transpose(1, 0, 2)
            wu = ffn_w_up[i].astype(BF16).reshape(D, nf, FFN_FC).transpose(1, 0, 2)
            wd = ffn_w_down[i].astype(BF16).reshape(nf, FFN_FC, D)
            xs = _ffn(h2, x1, modsel[layer], wg, wu, wd)
        else:
            etab, ctab = _moe_tables(cnt)
            ys = _moe_experts(etab, h2, rk[:, :N_EXPERTS].T, moe_w_gate[i:i + 1], moe_w_up[i:i + 1],
                              moe_w_down[i:i + 1])
            xs = _moe_combine(ctab[0], ctab[1], x1, gate, rk, modsel[layer, 1, 5][None, :], ys)
    return xs[None]
```

```python
import functools

import numpy as np
import jax
import jax.numpy as jnp
from jax import lax
from jax.experimental import pallas as pl
from jax.experimental.pallas import tpu as pltpu

F32 = jnp.float32
BF16 = jnp.bfloat16

D = 1024
SEQ = 16384
CTX = 256
ROWS = CTX + SEQ
DEPTH = 2
GRID_W = 64
EPS = 1e-6
HEAD_DIM = 64
N_Q_HEADS = 8
N_KV_HEADS = 2
ATTN_W = 512
KV_W = 128
WINDOW = 128
ROPE_THETA = 10000.0
HG_HEADS = 4
HG_K = 64
HG_W = 256
GM_GROUPS = 4
GM_DIM = 64
GM_W = 256
GM_CHUNK = 128
IN_W = 2560
FFN_DIM = 2816
N_EXPERTS = 8
EXPERT_DIM = 3584

VMEM_LIMIT = 56 * 1024 * 1024
NEG = -1e30

TM = 256
HT = 128
HG_SUB = 2
HG_LEVELS = (8, 16, 32, 64)
AB = 128
FFN_TM = 640
FFN_FC = 1408
MOE_TILE = 256
MOE_CAP = 17 * MOE_TILE
MOE_ITEMS = 2 * SEQ // MOE_CAP + N_EXPERTS
MOE_GC = 512
MOE_NG = SEQ // MOE_GC
MOE_FC = 512
MOE_NF = EXPERT_DIM // MOE_FC
MOE_CW = 256
MOE_CT = 128
MOE_CS = MOE_CW // MOE_CT + 1
MOE_VMEM_LIMIT = 60 * 1024 * 1024


def _params(sem, vmem=VMEM_LIMIT):
    return pltpu.CompilerParams(dimension_semantics=sem, vmem_limit_bytes=vmem)


def _block_ones(n, seg):
    i = np.arange(n)
    return (i[:, None] // seg == i[None, :] // seg).astype(np.float32)


def _ada_kernel(a_ref, w_ref, b_ref, o_ref):
    a = a_ref[...]
    s = (a * jax.nn.sigmoid(a)).astype(BF16)
    w = w_ref[0].astype(BF16)
    o_ref[0] = jnp.dot(s, w, preferred_element_type=F32) + b_ref[0]


def _ada(cc, w_ada, b_ada):
    tn = 1536
    return pl.pallas_call(
        _ada_kernel,
        grid=(DEPTH, 6 * D // tn),
        in_specs=[
            pl.BlockSpec((8, D), lambda l, n: (0, 0)),
            pl.BlockSpec((1, D, tn), lambda l, n: (l, 0, n)),
            pl.BlockSpec((1, 1, tn), lambda l, n: (l, 0, n)),
        ],
        out_specs=pl.BlockSpec((1, 8, tn), lambda l, n: (l, 0, n)),
        out_shape=jax.ShapeDtypeStruct((DEPTH, 8, 6 * D), F32),
        compiler_params=_params(("parallel", "parallel")),
        name="ada_mod",
    )(cc, w_ada, b_ada.reshape(DEPTH, 1, 6 * D))


def _gelu(x):
    return 0.5 * x * (1.0 + lax.erf(x * 0.7071067811865476))


def _inproj_kernel(x_ref, mod_ref, w_ref, cs_ref, sn_ref, gain_ref, bq_ref, bg_ref, gng_ref, ws_ref, bs_ref,
                   q_ref, k_ref, v_ref, ph_ref, gm_ref):
    x = x_ref[...]
    r = lax.rsqrt(jnp.mean(x * x, axis=-1, keepdims=True) + EPS)
    mod = mod_ref[0]
    h = (x * r) * (1.0 + mod[1:2]) + mod[0:1]
    p = jnp.dot(h.astype(BF16), w_ref[...], preferred_element_type=F32)

    qk = p[:, :ATTN_W + KV_W]
    ss = jnp.dot((qk * qk).astype(BF16), bq_ref[...], preferred_element_type=F32)
    qkn = qk * lax.rsqrt(ss * (1.0 / HEAD_DIM) + EPS) * gain_ref[...]
    lane = lax.broadcasted_iota(jnp.int32, (TM, 128), 1)
    first_half = (lane // 16) % 2 == 0
    cs = cs_ref[...]
    sn = sn_ref[...]
    rot = []
    for j in range(5):
        xs = qkn[:, 128 * j:128 * (j + 1)]
        partner = jnp.where(first_half, pltpu.roll(xs, 112, 1), pltpu.roll(xs, 16, 1))
        rot.append(xs * cs + partner * sn)
    q_ref[...] = jnp.concatenate(rot[:4], axis=1).astype(BF16)

    low = lane < 64

    def dup(t):
        sw = pltpu.roll(t, 64, 1)
        return jnp.concatenate([jnp.where(low, t, sw), jnp.where(low, sw, t)], axis=1)

    k_ref[...] = dup(rot[4]).astype(BF16)
    v_ref[...] = dup(p[:, ATTN_W + KV_W:ATTN_W + 2 * KV_W]).astype(BF16)

    ph_ref[...] = p[:, 768:2048]

    u = _gelu(p[:, 2048:2304])
    vv = _gelu(p[:, 2304:2560])
    ssv = jnp.dot((vv * vv).astype(BF16), bg_ref[...], preferred_element_type=F32)
    vn = vv * lax.rsqrt(ssv * (1.0 / GM_DIM) + EPS) * gng_ref[...]
    group = lax.broadcasted_iota(jnp.int32, (GM_CHUNK, GM_W), 1) // GM_DIM
    for c in range(TM // GM_CHUNK):
        vc = vn[GM_CHUNK * c:GM_CHUNK * (c + 1)]
        acc = bs_ref[...]
        for g in range(GM_GROUPS):
            vg = jnp.where(group == g, vc, 0.0).astype(BF16)
            acc = acc + jnp.dot(ws_ref[g], vg, preferred_element_type=F32)
        gm_ref[GM_CHUNK * c:GM_CHUNK * (c + 1), :] = (u[GM_CHUNK * c:GM_CHUNK * (c + 1)] * acc).astype(BF16)


def _inproj(xs, modsel, w_in, cs, sn, gain, bq, bg, gng, ws, bs):
    nt = ROWS // TM
    const2 = lambda i: (0, 0)
    return pl.pallas_call(
        _inproj_kernel,
        grid=(nt,),
        in_specs=[
            pl.BlockSpec((TM, D), lambda i: (i, 0)),
            pl.BlockSpec((1, 8, D), lambda i: (jnp.minimum(i, 1), 0, 0)),
            pl.BlockSpec((D, IN_W), const2),
            pl.BlockSpec((TM, 128), lambda i: (i, 0)),
            pl.BlockSpec((TM, 128), lambda i: (i, 0)),
            pl.BlockSpec((1, ATTN_W + KV_W), const2),
            pl.BlockSpec((ATTN_W + KV_W, ATTN_W + KV_W), const2),
            pl.BlockSpec((GM_W, GM_W), const2),
            pl.BlockSpec((1, GM_W), const2),
            pl.BlockSpec((GM_GROUPS, GM_CHUNK, GM_CHUNK), lambda i: (0, 0, 0)),
            pl.BlockSpec((GM_CHUNK, GM_W), const2),
        ],
        out_specs=[
            pl.BlockSpec((TM, ATTN_W), lambda i: (i, 0)),
            pl.BlockSpec((TM, 2 * KV_W), lambda i: (i, 0)),
            pl.BlockSpec((TM, 2 * KV_W), lambda i: (i, 0)),
            pl.BlockSpec((TM, 5 * HG_W), lambda i: (i, 0)),
            pl.BlockSpec((TM, GM_W), lambda i: (i, 0)),
        ],
        out_shape=[
            jax.ShapeDtypeStruct((ROWS, ATTN_W), BF16),
            jax.ShapeDtypeStruct((ROWS, 2 * KV_W), BF16),
            jax.ShapeDtypeStruct((ROWS, 2 * KV_W), BF16),
            jax.ShapeDtypeStruct((ROWS, 5 * HG_W), F32),
            jax.ShapeDtypeStruct((ROWS, GM_W), BF16),
        ],
        compiler_params=_params(("parallel",)),
        name="inproj",
    )(xs, modsel, w_in, cs, sn, gain, bq, bg, gng, ws, bs)


def _hgrn_consts():
    t = np.arange(HT)
    tt, ss = t[:, None], t[None, :]
    stack, masks = [], []
    for m in HG_LEVELS:
        same = (tt // m) == (ss // m)
        stack += [same & (ss <= tt), same & (ss > tt)]
        masks.append(((tt // (2 * m)) == (ss // (2 * m))) & ((tt % (2 * m)) >= m) & ((ss % (2 * m)) < m))
    stack.append(ss <= tt)
    masks.append(((tt // 8) == (ss // 8)) & (ss <= tt))
    lst_f = np.concatenate(stack, axis=0).astype(np.float32)
    lst_b = np.concatenate([b.T for b in stack], axis=0).astype(np.float32)
    msk_f = np.stack([np.tile(m, (1, HG_HEADS)) for m in masks]).astype(np.float32)
    msk_b = np.stack([np.tile(m.T, (1, HG_HEADS)) for m in masks]).astype(np.float32)
    e = np.zeros((8, HG_W, 128), np.float32)
    tl = np.zeros((128, HG_HEADS * HT), np.float32)
    for h in range(HG_HEADS):
        for j in range(8):
            e[j, h * HG_K:(h + 1) * HG_K, h * 8 + j] = 1.0
            tl[h * 8 + j, h * HT + t[(t % 8) == j]] = 1.0
    return lst_f, lst_b, msk_f, msk_b, e.reshape(8 * HG_W, 128), tl, _block_ones(HG_W, HG_K)


def _hgrn_direction(zq, zf, vi, lb, lst_ref, msk_ref, e_ref, tl_ref, bd_ref, st_ref, q3_ref, k3_ref, p3_ref):
    q = (zq * jax.nn.sigmoid(zq)) * (HG_K ** -0.5)
    f = lb + (1.0 - lb) * jax.nn.sigmoid(zf)
    g = jnp.log(f)
    kk = 1.0 - f
    g_hi = g.astype(BF16)
    g_lo = (g - g_hi.astype(F32)).astype(BF16)
    cum2 = jnp.dot(lst_ref[...], jnp.concatenate([g_hi, g_lo], axis=1), preferred_element_type=F32)
    cum = cum2[:, :HG_W] + cum2[:, HG_W:]
    nl = len(HG_LEVELS)
    c = cum[2 * nl * HT:(2 * nl + 1) * HT]
    ctot = jnp.sum(g, axis=0, keepdims=True)
    head = lax.broadcasted_iota(jnp.int32, (HT, HG_W), 1) // HG_K
    nt_dims = (((1,), (1,)), ((), ()))

    def per_head_rows(t):
        return jnp.concatenate([jnp.where(head == h, t, 0.0).astype(BF16) for h in range(HG_HEADS)], axis=0)

    yield
    st = st_ref[...]
    qe = (q * jnp.exp(c)).astype(BF16)
    o = lax.dot_general(qe, st.astype(BF16), nt_dims, preferred_element_type=F32)
    ke = (kk * jnp.exp(ctot - c)).astype(BF16)
    upd = lax.dot_general(vi.astype(BF16), ke, (((0,), (0,)), ((), ())), preferred_element_type=F32)
    st_ref[...] = st * jnp.exp(ctot) + upd * bd_ref[...]
    yield

    p8 = cum[0:HT]
    q3_ref[...] = q.reshape(HT // 8, 8, HG_W)
    k3_ref[...] = kk.reshape(HT // 8, 8, HG_W)
    p3_ref[...] = p8.reshape(HT // 8, 8, HG_W)
    slabs = []
    for j in range(8):
        kj = k3_ref[:, j:j + 1, :]
        pj = p3_ref[:, j:j + 1, :]
        slab = q3_ref[...] * kj * jnp.exp(jnp.minimum(p3_ref[...] - pj, 0.0))
        slabs.append(slab.reshape(HT, HG_W).astype(BF16))
        if j % 2 == 1:
            yield
    a = jnp.dot(jnp.concatenate(slabs, axis=1), e_ref[...], preferred_element_type=F32)
    sc = msk_ref[nl] * jnp.dot(a.astype(BF16), tl_ref[...], preferred_element_type=F32)

    for li in range(nl):
        yield
        pm = cum[2 * li * HT:(2 * li + 1) * HT]
        rm = cum[(2 * li + 1) * HT:(2 * li + 2) * HT]
        qt = (q * jnp.exp(pm)).astype(BF16)
        kt = per_head_rows(kk * jnp.exp(rm))
        sc = sc + msk_ref[li] * lax.dot_general(qt, kt, nt_dims, preferred_element_type=F32)

    yield
    return o + jnp.dot(sc.astype(BF16), per_head_rows(vi), preferred_element_type=F32)


def _hgrn_kernel(qf_ref, ff_ref, if_ref, qb_ref, fb_ref, ib_ref, lbf_ref, lbb_ref,
                 lstf_ref, lstb_ref, mskf_ref, mskb_ref, e_ref, tl_ref, bd_ref,
                 of_ref, ob_ref, stf_ref, stb_ref, *scr):
    @pl.when(pl.program_id(0) == 0)
    def _():
        stf_ref[...] = jnp.zeros_like(stf_ref)
        stb_ref[...] = jnp.zeros_like(stb_ref)

    scans, dests = [], []
    for k in range(HG_SUB):
        rf = slice(k * HT, (k + 1) * HT)
        rb = slice((HG_SUB - 1 - k) * HT, (HG_SUB - k) * HT)
        s3 = scr[6 * k:6 * (k + 1)]
        scans.append(_hgrn_direction(qf_ref[rf, :], ff_ref[rf, :], if_ref[rf, :], lbf_ref[...], lstf_ref, mskf_ref,
                                     e_ref, tl_ref, bd_ref, stf_ref, *s3[:3]))
        dests.append((of_ref, rf))
        scans.append(_hgrn_direction(qb_ref[rb, :], fb_ref[rb, :], ib_ref[rb, :], lbb_ref[...], lstb_ref, mskb_ref,
                                     e_ref, tl_ref, bd_ref, stb_ref, *s3[3:]))
        dests.append((ob_ref, rb))
    live = list(range(len(scans)))
    while live:
        for d in list(live):
            try:
                next(scans[d])
            except StopIteration as done:
                ref, rows = dests[d]
                ref[rows, :] = done.value
                live.remove(d)


def _hgrn(ph, lbf, lbb, consts):
    lst_f, lst_b, msk_f, msk_b, e, tl, bd = consts
    rows = HG_SUB * HT
    nt = ROWS // rows
    nc = CTX // rows

    def fwd(col):
        return pl.BlockSpec((rows, HG_W), lambda j: (j, col))

    def bwd(col):
        return pl.BlockSpec((rows, HG_W), lambda j: (jnp.where(j < nc, nc - 1 - j, nt + nc - 1 - j), col))

    def const(a):
        nd = a.ndim
        return pl.BlockSpec(a.shape, lambda j: (0,) * nd)

    scr3 = pltpu.VMEM((HT // 8, 8, HG_W), F32)
    return pl.pallas_call(
        _hgrn_kernel,
        grid=(nt,),
        in_specs=[fwd(0), fwd(1), fwd(3), bwd(0), bwd(2), bwd(3), const(lbf), const(lbb),
                  const(lst_f), const(lst_b), const(msk_f), const(msk_b), const(e), const(tl), const(bd)],
        out_specs=[pl.BlockSpec((rows, HG_W), lambda j: (j, 0)),
                   pl.BlockSpec((rows, HG_W), lambda j: (jnp.where(j < nc, nc - 1 - j, nt + nc - 1 - j), 0))],
        out_shape=[jax.ShapeDtypeStruct((ROWS, HG_W), F32), jax.ShapeDtypeStruct((ROWS, HG_W), F32)],
        scratch_shapes=[pltpu.VMEM((HG_W, HG_W), F32), pltpu.VMEM((HG_W, HG_W), F32)] + [scr3] * (6 * HG_SUB),
        compiler_params=_params(("arbitrary",)),
        name="hgrn_scan",
    )(ph, ph, ph, ph, ph, ph, lbf, lbb, lst_f, lst_b, msk_f, msk_b, e, tl, bd)


def _attn_core(q, kcat, vcat, bias, sink_ref, hm_ref, o_ref):
    nk = kcat.shape[0]
    nt_dims = (((1,), (1,)), ((), ()))
    pairs = []
    for g in range(N_KV_HEADS):
        kg = kcat[:, 128 * g:128 * (g + 1)]
        vg = vcat[:, 128 * g:128 * (g + 1)]
        vsel = (vg * hm_ref[0, :nk, :], vg * hm_ref[1, :nk, :])
        qs = []
        for pr in (2 * g, 2 * g + 1):
            qp = q[:, 128 * pr:128 * (pr + 1)]
            qs += [qp * hm_ref[0, :AB, :], qp * hm_ref[1, :AB, :]]
        s = lax.dot_general(jnp.concatenate(qs, axis=0), kg, nt_dims, preferred_element_type=F32)
        for pr in range(2):
            acc = None
            for a in range(2):
                hh = 2 * pr + a
                sk = sink_ref[4 * g + hh]
                sh = s[AB * hh:AB * (hh + 1)]
                if bias is not None:
                    sh = sh + bias
                m = jnp.maximum(jnp.max(sh, axis=1, keepdims=True), sk)
                e = jnp.exp(sh - m)
                den = jnp.sum(e, axis=1, keepdims=True) + jnp.exp(sk - m)
                pv = jnp.dot(e.astype(BF16), vsel[a], preferred_element_type=F32) * (1.0 / den)
                acc = pv if acc is None else acc + pv
            pairs.append(acc)
    o_ref[...] = jnp.concatenate(pairs, axis=1).astype(BF16)


def _attn_lat_kernel(q_ref, km_ref, k0_ref, kp_ref, kc_ref, vm_ref, v0_ref, vp_ref, vc_ref, sink_ref, hm_ref, o_ref):
    i = pl.program_id(0)
    nb = pl.num_programs(0)
    kcat = jnp.concatenate([km_ref[...], k0_ref[...], kp_ref[...], kc_ref[...]], axis=0)
    vcat = jnp.concatenate([vm_ref[...], v0_ref[...], vp_ref[...], vc_ref[...]], axis=0)
    nk = 3 * AB + CTX
    a = lax.broadcasted_iota(jnp.int32, (AB, nk), 0)
    r = lax.broadcasted_iota(jnp.int32, (AB, nk), 1)
    lo = jnp.where(i == 0, AB, 0)
    hi = jnp.where(i == nb - 1, 2 * AB - 1, 3 * AB - 1)
    lower = jnp.maximum(a + (AB - WINDOW), lo)
    upper = jnp.minimum(a + (AB + WINDOW), hi)
    in_band = jnp.where(r >= lower, jnp.where(r <= upper, 0.0, NEG), NEG)
    bias = jnp.where(r >= 3 * AB, 0.0, in_band)
    _attn_core(q_ref[...], kcat, vcat, bias, sink_ref, hm_ref, o_ref)


def _attn_ctx_kernel(q_ref, kc_ref, vc_ref, sink_ref, hm_ref, o_ref):
    _attn_core(q_ref[...], kc_ref[...], vc_ref[...], None, sink_ref, hm_ref, o_ref)


def _head_masks():
    hm = np.zeros((2, 3 * AB + CTX, 128), np.float32)
    hm[0, :, :64] = 1.0
    hm[1, :, 64:] = 1.0
    return hm


def _attn_lat(q, k, v, sink, hm):
    nb = SEQ // AB
    off = CTX // AB
    last = ROWS // AB - 1
    kv = lambda f: pl.BlockSpec((AB, 2 * KV_W), f)
    prev = lambda i: (i + off - 1, 0)
    cur = lambda i: (i + off, 0)
    nxt = lambda i: (jnp.minimum(i + off + 1, last), 0)
    ctx = pl.BlockSpec((CTX, 2 * KV_W), lambda i: (0, 0))
    return pl.pallas_call(
        _attn_lat_kernel,
        grid=(nb,),
        in_specs=[pl.BlockSpec((AB, ATTN_W), cur), kv(prev), kv(cur), kv(nxt), ctx,
                  kv(prev), kv(cur), kv(nxt), ctx,
                  pl.BlockSpec(memory_space=pltpu.SMEM),
                  pl.BlockSpec((2, 3 * AB + CTX, 128), lambda i: (0, 0, 0))],
        out_specs=pl.BlockSpec((AB, ATTN_W), lambda i: (i, 0)),
        out_shape=jax.ShapeDtypeStruct((SEQ, ATTN_W), BF16),
        compiler_params=_params(("parallel",)),
        name="attn_window",
    )(q, k, k, k, k, v, v, v, v, sink, hm)


def _attn_ctx(q, k, v, sink, hm):
    ctx = pl.BlockSpec((CTX, 2 * KV_W), lambda i: (0, 0))
    return pl.pallas_call(
        _attn_ctx_kernel,
        grid=(CTX // AB,),
        in_specs=[pl.BlockSpec((AB, ATTN_W), lambda i: (i, 0)), ctx, ctx,
                  pl.BlockSpec(memory_space=pltpu.SMEM),
                  pl.BlockSpec((2, 3 * AB + CTX, 128), lambda i: (0, 0, 0))],
        out_specs=pl.BlockSpec((AB, ATTN_W), lambda i: (i, 0)),
        out_shape=jax.ShapeDtypeStruct((CTX, ATTN_W), BF16),
        compiler_params=_params(("parallel",)),
        name="attn_context",
    )(q, k, v, sink, hm)


def _outproj_kernel(*refs, with_ctx, with_router):
    it = iter(refs)
    x_ref = next(it)
    mod_ref = next(it)
    al_ref = next(it)
    ac_ref = next(it) if with_ctx else None
    of_ref, ob_ref, hg_ref, gm_ref, w_ref, hgain_ref, bh_ref = (next(it) for _ in range(7))
    rt_ref = next(it) if with_router else None
    lt_ref = next(it) if with_router else None
    x1_ref = next(it)
    h2_ref = next(it)
    if with_router:
        gate_ref, rank_ref, cnt_ref, base_ref = (next(it) for _ in range(4))

    mod = mod_ref[0]
    attn = al_ref[...]
    if with_ctx:
        attn = jnp.where(pl.program_id(0) == 0, ac_ref[...], attn)
    o = of_ref[...] + ob_ref[...]
    ss = jnp.dot((o * o).astype(BF16), bh_ref[...], preferred_element_type=F32)
    hg = hg_ref[...]
    y = o * lax.rsqrt(ss * (1.0 / HG_K) + EPS) * hgain_ref[...] * (hg * jax.nn.sigmoid(hg))
    mix = jnp.concatenate([attn, y.astype(BF16), gm_ref[...]], axis=1)
    proj = jnp.dot(mix, w_ref[...], preferred_element_type=F32)
    x1 = x_ref[...] + mod[2:3] * proj
    x1_ref[...] = x1
    r = lax.rsqrt(jnp.mean(x1 * x1, axis=-1, keepdims=True) + EPS)
    h2 = (x1 * r) * (1.0 + mod[4:5]) + mod[3:4]
    h2_hi = h2.astype(BF16)
    h2_ref[...] = h2_hi

    if with_router:
        h2_lo = (h2 - h2_hi.astype(F32)).astype(BF16)
        two = jnp.dot(h2_hi, rt_ref[...], preferred_element_type=F32)
        logits = two[:, :128] + two[:, 128:] + jnp.dot(h2_lo, rt_ref[:, :128], preferred_element_type=F32)
        lane = lax.broadcasted_iota(jnp.int32, logits.shape, 1)
        l1 = jnp.where(lane < N_EXPERTS, logits, NEG)
        m1 = jnp.max(l1, axis=1, keepdims=True)
        i1 = jnp.min(jnp.where(l1 == m1, lane, 128), axis=1, keepdims=True)
        l2 = jnp.where(lane == i1, NEG, l1)
        m2 = jnp.max(l2, axis=1, keepdims=True)
        i2 = jnp.min(jnp.where(l2 == m2, lane, 128), axis=1, keepdims=True)
        e2 = jnp.exp(m2 - m1)
        w1 = 1.0 / (1.0 + e2)
        gate_ref[...] = jnp.where(lane == i1, w1, 0.0) + jnp.where(lane == i2, e2 * w1, 0.0)

        @pl.when(pl.program_id(0) == 0)
        def _():
            base_ref[...] = jnp.zeros_like(base_ref)

        mem = jnp.where(lane == i1, 1.0, jnp.where(lane == i2, 1.0, 0.0))
        local = jnp.dot(lt_ref[...], mem.astype(BF16), preferred_element_type=F32)
        base = base_ref[...]
        rank_ref[...] = jnp.where(mem > 0.0, base + local, -1.0)
        after = base + jnp.sum(mem, axis=0, keepdims=True)
        cnt_ref[0, 0:1, :] = base
        cnt_ref[0, 1:2, :] = after
        cnt_ref[0, 2:8, :] = jnp.zeros((6, 128), F32)
        base_ref[...] = after


def _outproj(xs, modsel, attn_lat, attn_ctx, o_f, o_b, ph, gm, w_out, hgain, bh, router, *, with_ctx):
    with_router = router is not None
    off = 0 if with_ctx else CTX // TM
    nt = ROWS // TM - off
    rows_out = ROWS if with_ctx else SEQ
    row = lambda w: pl.BlockSpec((TM, w), lambda i: (i + off, 0))
    const2 = lambda i: (0, 0)
    in_specs = [row(D),
                pl.BlockSpec((1, 8, D), lambda i: (jnp.minimum(i + off, 1), 0, 0)),
                pl.BlockSpec((TM, ATTN_W), lambda i: (jnp.maximum(i + off - CTX // TM, 0), 0))]
    args = [xs, modsel, attn_lat]
    if with_ctx:
        in_specs.append(pl.BlockSpec((TM, ATTN_W), const2))
        args.append(attn_ctx)
    in_specs += [row(HG_W), row(HG_W),
                 pl.BlockSpec((TM, HG_W), lambda i: (i + off, 4)),
                 row(GM_W),
                 pl.BlockSpec((D, D), const2),
                 pl.BlockSpec((1, HG_W), const2),
                 pl.BlockSpec((HG_W, HG_W), const2)]
    args += [o_f, o_b, ph, gm, w_out, hgain, bh]
    out_specs = [pl.BlockSpec((TM, D), lambda i: (i, 0)), pl.BlockSpec((TM, D), lambda i: (i, 0))]
    out_shape = [jax.ShapeDtypeStruct((rows_out, D), F32), jax.ShapeDtypeStruct((rows_out, D), BF16)]
    scratch = []
    if with_router:
        t = np.arange(TM)
        strict_lower = jnp.asarray((t[None, :] < t[:, None]).astype(np.float32), BF16)
        in_specs += [pl.BlockSpec((D, 256), const2), pl.BlockSpec((TM, TM), const2)]
        args += [router, strict_lower]
        out_specs += [pl.BlockSpec((TM, 128), lambda i: (i, 0)), pl.BlockSpec((TM, 128), lambda i: (i, 0)),
                      pl.BlockSpec((1, 8, 128), lambda i: (i, 0, 0))]
        out_shape += [jax.ShapeDtypeStruct((rows_out, 128), F32), jax.ShapeDtypeStruct((rows_out, 128), F32),
                      jax.ShapeDtypeStruct((nt, 8, 128), F32)]
        scratch = [pltpu.VMEM((1, 128), F32)]
    return pl.pallas_call(
        functools.partial(_outproj_kernel, with_ctx=with_ctx, with_router=with_router),
        grid=(nt,),
        in_specs=in_specs,
        out_specs=out_specs,
        out_shape=out_shape,
        scratch_shapes=scratch,
        compiler_params=_params(("arbitrary",) if with_router else ("parallel",)),
        name="outproj",
    )(*args)


def _ffn_kernel(h_ref, x_ref, mod_ref, wg_ref, wu_ref, wd_ref, o_ref):
    h = h_ref[...]
    acc = None
    for f in range(FFN_DIM // FFN_FC):
        cols = slice(f * FFN_FC, (f + 1) * FFN_FC)
        a = jnp.dot(h, wg_ref[:, cols], preferred_element_type=F32)
        b = jnp.dot(h, wu_ref[:, cols], preferred_element_type=F32)
        t = (a * jax.nn.sigmoid(a) * b).astype(BF16)
        y = jnp.dot(t, wd_ref[cols, :], preferred_element_type=F32)
        acc = y if acc is None else acc + y
    row = pl.program_id(0) * FFN_TM + lax.broadcasted_iota(jnp.int32, (FFN_TM, D), 0)
    gate = jnp.where(row < CTX, mod_ref[0][5:6], mod_ref[1][5:6])
    o_ref[...] = x_ref[...] + gate * acc


def _ffn(h2, x1, modsel, wg, wu, wd):
    nt = ROWS // FFN_TM
    resident = dict(pipeline_mode=pl.Buffered(1))
    return pl.pallas_call(
        _ffn_kernel,
        grid=(nt,),
        in_specs=[pl.BlockSpec((FFN_TM, D), lambda i: (i, 0)),
                  pl.BlockSpec((FFN_TM, D), lambda i: (i, 0)),
                  pl.BlockSpec((2, 8, D), lambda i: (0, 0, 0)),
                  pl.BlockSpec((D, FFN_DIM), lambda i: (0, 0), **resident),
                  pl.BlockSpec((D, FFN_DIM), lambda i: (0, 0), **resident),
                  pl.BlockSpec((FFN_DIM, D), lambda i: (0, 0), **resident)],
        out_specs=pl.BlockSpec((FFN_TM, D), lambda i: (i, 0)),
        out_shape=jax.ShapeDtypeStruct((ROWS, D), F32),
        compiler_params=_params(("parallel",)),
        name="ffn_dense",
    )(h2, x1, modsel, wg, wu, wd)


def _moe_index(i, s, ie, ibase, irows, cnt):
    valid = irows[i] > 0
    chunk = jnp.where(valid, jnp.minimum(s, MOE_NG - 1), MOE_NG - 1)
    fidx = jnp.where(valid, jnp.maximum(s - MOE_NG, 0), MOE_NF - 1)
    return chunk, fidx


def _moe_expert_kernel(ie_ref, ibase_ref, irows_ref, cnt_ref,
                       h_ref, rk_ref, wg_ref, wu_ref, wd_ref, ys_ref, yg_ref):
    i = pl.program_id(0)
    s = pl.program_id(1)
    e = ie_ref[i]
    base = ibase_ref[i]
    rows = irows_ref[i]
    ntile = (rows + MOE_TILE - 1) // MOE_TILE

    @pl.when(s == 0)
    def _():
        ys_ref[...] = jnp.zeros_like(ys_ref)

    @pl.when((s < MOE_NG) & (rows > 0))
    def _():
        lo = jnp.maximum(cnt_ref[s * N_EXPERTS + e] - base, 0)
        hi = jnp.minimum(cnt_ref[(s + 1) * N_EXPERTS + e] - base, rows) - 1
        a_lo = lo // MOE_TILE
        a_hi = jnp.where(hi >= lo, hi // MOE_TILE, a_lo - 1)
        rel = rk_ref[pl.ds(e, 1), :].astype(jnp.int32) - base
        h = h_ref[...]

        def tile_body(a, carry):
            off = pl.multiple_of(a * MOE_TILE, MOE_TILE)
            slot = lax.broadcasted_iota(jnp.int32, (MOE_TILE, MOE_GC), 0) + off
            onehot = jnp.where(rel == slot, 1.0, 0.0).astype(BF16)
            packed = jnp.dot(onehot, h, preferred_element_type=F32)
            ys_ref[pl.ds(off, MOE_TILE), :] = (ys_ref[pl.ds(off, MOE_TILE), :].astype(F32) + packed).astype(BF16)
            return carry

        lax.fori_loop(a_lo, a_hi + 1, tile_body, 0)

    def expert_step(mode):
        wg = wg_ref[0, 0].astype(BF16)
        wu = wu_ref[0, 0].astype(BF16)
        wd = wd_ref[0, 0].astype(BF16)

        def tile_body(k, carry):
            off = pl.multiple_of(k * MOE_TILE, MOE_TILE)
            xg = ys_ref[pl.ds(off, MOE_TILE), :]
            a = jnp.dot(xg, wg, preferred_element_type=F32)
            b = jnp.dot(xg, wu, preferred_element_type=F32)
            y = jnp.dot((a * jax.nn.sigmoid(a) * b).astype(BF16), wd, preferred_element_type=F32)
            if mode == "first":
                yg_ref[pl.ds(off, MOE_TILE), :] = y
            elif mode == "middle":
                yg_ref[pl.ds(off, MOE_TILE), :] += y
            else:
                ys_ref[pl.ds(off, MOE_TILE), :] = (yg_ref[pl.ds(off, MOE_TILE), :] + y).astype(BF16)
            return carry

        lax.fori_loop(0, ntile, tile_body, 0)

    f = s - MOE_NG
    pl.when((f == 0) & (rows > 0))(functools.partial(expert_step, "first"))
    pl.when((f > 0) & (f < MOE_NF - 1) & (rows > 0))(functools.partial(expert_step, "middle"))
    pl.when((f == MOE_NF - 1) & (rows > 0))(functools.partial(expert_step, "last"))


def _moe_experts(tables, h2, rk_t, wg, wu, wd):
    ie, ibase, irows, cnt = tables

    def hmap(i, s, ie, ibase, irows, cnt):
        return (_moe_index(i, s, ie, ibase, irows, cnt)[0], 0)

    def rmap(i, s, ie, ibase, irows, cnt):
        return (0, _moe_index(i, s, ie, ibase, irows, cnt)[0])

    def upmap(i, s, ie, ibase, irows, cnt):
        return (0, ie[i], 0, _moe_index(i, s, ie, ibase, irows, cnt)[1])

    def downmap(i, s, ie, ibase, irows, cnt):
        return (0, ie[i], _moe_index(i, s, ie, ibase, irows, cnt)[1], 0)

    grid_spec = pltpu.PrefetchScalarGridSpec(
        num_scalar_prefetch=4,
        grid=(MOE_ITEMS, MOE_NG + MOE_NF),
        in_specs=[pl.BlockSpec((MOE_GC, D), hmap),
                  pl.BlockSpec((N_EXPERTS, MOE_GC), rmap),
                  pl.BlockSpec((1, 1, D, MOE_FC), upmap),
                  pl.BlockSpec((1, 1, D, MOE_FC), upmap),
                  pl.BlockSpec((1, 1, MOE_FC, D), downmap)],
        out_specs=pl.BlockSpec((MOE_CAP, D), lambda i, s, ie, ibase, irows, cnt: (i, 0)),
        scratch_shapes=[pltpu.VMEM((MOE_CAP, D), F32)],
    )
    return pl.pallas_call(
        _moe_expert_kernel,
        grid_spec=grid_spec,
        out_shape=jax.ShapeDtypeStruct((MOE_ITEMS * MOE_CAP, D), BF16),
        compiler_params=_params(("arbitrary", "arbitrary"), vmem=MOE_VMEM_LIMIT),
        name="moe_experts",
    )(ie, ibase, irows, cnt, h2, rk_t, wg, wu, wd)


def _moe_combine_kernel(tix_ref, se_ref, f0_ref, f1_ref, f2_ref, x_ref, gate_ref, rk_ref, gf_ref, *refs):
    y_refs, o_ref = refs[:-1], refs[-1]
    w = pl.program_id(0)
    gate = gate_ref[...]
    rk = rk_ref[...]
    lane = lax.broadcasted_iota(jnp.int32, gate.shape, 1)
    slot = lax.broadcasted_iota(jnp.int32, (MOE_CW, MOE_CS * MOE_CT), 1)
    acc = jnp.zeros((MOE_CW, D), F32)
    for e in range(N_EXPERTS):
        rcol = jnp.sum(jnp.where(lane == e, rk, 0.0), axis=1, keepdims=True)
        gcol = jnp.sum(jnp.where(lane == e, gate, 0.0), axis=1, keepdims=True)
        pos = rcol.astype(jnp.int32) + (se_ref[e] - tix_ref[w * N_EXPERTS + e] * MOE_CT)
        pos = jnp.where(rcol < 0.0, -1, pos)
        onehot = jnp.where(pos == slot, 1.0, 0.0).astype(BF16)
        ycat = jnp.concatenate([y_refs[MOE_CS * e + j][...] for j in range(MOE_CS)], axis=0)
        acc = acc + gcol * jnp.dot(onehot, ycat, preferred_element_type=F32)
    o_ref[...] = x_ref[...] + gf_ref[...] * acc


def _moe_combine(ctab, x1, gate, rk, gf, ys):
    assert MOE_CS == 3
    row = lambda wd_: pl.BlockSpec((MOE_CW, wd_), lambda w, *tabs: (w, 0))

    def ymap(e, j):
        return lambda w, *tabs: (tabs[2 + j][w * N_EXPERTS + e], 0)

    y_specs = [pl.BlockSpec((MOE_CT, D), ymap(e, j)) for e in range(N_EXPERTS) for j in range(MOE_CS)]
    grid_spec = pltpu.PrefetchScalarGridSpec(
        num_scalar_prefetch=len(ctab),
        grid=(SEQ // MOE_CW,),
        in_specs=[row(D), row(128), row(128), pl.BlockSpec((1, D), lambda w, *tabs: (0, 0))] + y_specs,
        out_specs=row(D),
    )
    return pl.pallas_call(
        _moe_combine_kernel,
        grid_spec=grid_spec,
        out_shape=jax.ShapeDtypeStruct((SEQ, D), F32),
        compiler_params=_params(("arbitrary",)),
        name="moe_combine",
    )(*ctab, x1, gate, rk, gf, *([ys] * (N_EXPERTS * MOE_CS)))


def _moe_tables(cnt):
    before = cnt[:, 0, :N_EXPERTS].astype(jnp.int32)
    total = cnt[-1, 1, :N_EXPERTS].astype(jnp.int32)
    nit = (total + MOE_CAP - 1) // MOE_CAP
    ends = jnp.cumsum(nit)
    first = ends - nit
    item = jnp.arange(MOE_ITEMS, dtype=jnp.int32)
    n_items = ends[-1]
    ie = jnp.minimum(jnp.sum((ends[None, :] <= item[:, None]).astype(jnp.int32), axis=1), N_EXPERTS - 1)
    last_e = ie[jnp.maximum(n_items - 1, 0)]
    valid = item < n_items
    ibase = jnp.where(valid, (item - first[ie]) * MOE_CAP, 0)
    irows = jnp.where(valid, jnp.clip(total[ie] - ibase, 0, MOE_CAP), 0)
    ie = jnp.where(valid, ie, last_e)
    per_chunk = MOE_GC // TM
    cnt_g = jnp.concatenate([before[::per_chunk], total[None, :]], axis=0).reshape(-1)
    se = first * MOE_CAP
    per_win = MOE_CW // TM
    win_before = before[::per_win]
    win_rows = jnp.concatenate([win_before[1:], total[None, :]], axis=0) - win_before
    start = se[None, :] + win_before
    tix = jnp.minimum(start // MOE_CT, n_items * (MOE_CAP // MOE_CT) - MOE_CS)
    tix = jnp.maximum(tix, 0)
    end = start - tix * MOE_CT + win_rows
    nw = tix.shape[0]
    earlier = (jnp.arange(nw)[None, :] <= jnp.arange(nw)[:, None])[:, :, None]
    fetch = []
    for j in range(MOE_CS):
        want = jnp.where(end > j * MOE_CT, tix + j, 0)
        fetch.append(jnp.max(jnp.where(earlier, want[None, :, :], 0), axis=1).reshape(-1).astype(jnp.int32))
    etab = (ie, ibase.astype(jnp.int32), irows.astype(jnp.int32), cnt_g)
    ctab = (tix.reshape(-1).astype(jnp.int32), se.astype(jnp.int32)) + tuple(fetch)
    return etab, ctab


def _rope_tables():
    rows = SEQ // GRID_W
    row = np.repeat(np.arange(rows, dtype=np.float32), GRID_W)
    col = np.tile(np.arange(GRID_W, dtype=np.float32), rows)
    axis_dim = HEAD_DIM // 2
    inv_freq = (np.float32(ROPE_THETA) ** (-np.arange(0, axis_dim, 2, dtype=np.float32) / np.float32(axis_dim)))
    ang_r = row[:, None] * inv_freq.astype(np.float32)
    ang_c = col[:, None] * inv_freq.astype(np.float32)
    cos_r, sin_r, cos_c, sin_c = np.cos(ang_r), np.sin(ang_r), np.cos(ang_c), np.sin(ang_c)
    cs = np.concatenate([cos_r, cos_r, cos_c, cos_c], axis=1)
    sn = np.concatenate([-sin_r, sin_r, -sin_c, sin_c], axis=1)
    cs = np.concatenate([np.ones((CTX, HEAD_DIM), np.float32), cs], axis=0)
    sn = np.concatenate([np.zeros((CTX, HEAD_DIM), np.float32), sn], axis=0)
    return (jnp.asarray(np.tile(cs, (1, 2)).astype(np.float32)), jnp.asarray(np.tile(sn, (1, 2)).astype(np.float32)))


def kernel(x, c, ctx, c_ctx, w_ada, b_ada, w_in, w_out, q_norm_gain, k_norm_gain, attn_sink, hgrn_lower_bound, hgrn_out_gain, gmlp_w_s, gmlp_b_s, gmlp_norm_gain, ffn_w_gate, ffn_w_up, ffn_w_down, moe_router, moe_w_gate, moe_w_up, moe_w_down):
    cc = jnp.concatenate([c, c_ctx[None, :], jnp.zeros((6, D), F32)], axis=0)
    mod = _ada(cc, w_ada, b_ada)
    mod6 = mod[:, :2].reshape(DEPTH, 2, 6, D)
    modsel = jnp.pad(jnp.stack([mod6[:, 1], mod6[:, 0]], axis=1), ((0, 0), (0, 0), (0, 2), (0, 0)))

    lb_soft = jax.nn.softmax(hgrn_lower_bound.astype(F32), axis=1)
    lower = jnp.cumsum(lb_soft, axis=1) - lb_soft[:, :1]

    cs, sn = _rope_tables()
    hconsts = _hgrn_consts()
    hconsts = tuple(jnp.asarray(a, BF16) for a in hconsts[:2]) + tuple(jnp.asarray(a, F32) for a in hconsts[2:4]) \
        + tuple(jnp.asarray(a, BF16) for a in hconsts[4:6]) + (jnp.asarray(hconsts[6], F32),)
    bq = jnp.asarray(_block_ones(ATTN_W + KV_W, HEAD_DIM), BF16)
    bg = jnp.asarray(_block_ones(GM_W, GM_DIM), BF16)
    hm = jnp.asarray(_head_masks(), BF16)

    xs = jnp.concatenate([ctx[0], x[0]], axis=0)
    for layer in range(DEPTH):
        last = layer == DEPTH - 1
        gain = jnp.concatenate([jnp.tile(q_norm_gain[layer], N_Q_HEADS) * (HEAD_DIM ** -0.5),
                                jnp.tile(k_norm_gain[layer], N_KV_HEADS)])[None, :]
        bias = jnp.repeat(gmlp_b_s[layer].T, GM_DIM, axis=1)
        q, k, v, ph, gm = _inproj(xs, modsel[layer], w_in[layer].astype(BF16), cs, sn, gain, bq, bg,
                                  gmlp_norm_gain[layer][None, :], gmlp_w_s[layer].astype(BF16), bias)
        o_f, o_b = _hgrn(ph, lower[0, layer][None, :], lower[1, layer][None, :], hconsts)
        sink = attn_sink[layer]
        attn_lat = _attn_lat(q, k, v, sink, hm)
        hgain = jnp.tile(hgrn_out_gain[layer], HG_HEADS)[None, :]
        w_o = w_out[layer].astype(BF16)
        if not last:
            attn_ctx = _attn_ctx(q, k, v, sink, hm)
            x1, h2 = _outproj(xs, modsel[layer], attn_lat, attn_ctx, o_f, o_b, ph, gm, w_o, hgain, bg, None,
                              with_ctx=True)
        else:
            r32 = jnp.pad(moe_router[layer // 2], ((0, 0), (0, 128 - N_EXPERTS)))
            r_hi = r32.astype(BF16)
            router = jnp.concatenate([r_hi, (r32 - r_hi.astype(F32)).astype(BF16)], axis=1)
            x1, h2, gate, rk, cnt = _outproj(xs, modsel[layer], attn_lat, None, o_f, o_b, ph, gm, w_o, hgain, bg,
                                             router, with_ctx=False)
        i = layer // 2
        if layer % 2 == 0:
            xs = _ffn(h2, x1, modsel[layer], ffn_w_gate[i].astype(BF16), ffn_w_up[i].astype(BF16),
                      ffn_w_down[i].astype(BF16))
        else:
            etab, ctab = _moe_tables(cnt)
            ys = _moe_experts(etab, h2, rk[:, :N_EXPERTS].T, moe_w_gate[i:i + 1], moe_w_up[i:i + 1],
                              moe_w_down[i:i + 1])
            xs = _moe_combine(ctab, x1, gate, rk, modsel[layer, 1, 5][None, :], ys)
    return xs[None]
```

```python
import functools

import numpy as np
import jax
import jax.numpy as jnp
from jax import lax
from jax.experimental import pallas as pl
from jax.experimental.pallas import tpu as pltpu

F32 = jnp.float32
BF16 = jnp.bfloat16

D = 1024
SEQ = 16384
CTX = 256
ROWS = CTX + SEQ
DEPTH = 2
GRID_W = 64
EPS = 1e-6
HEAD_DIM = 64
N_Q_HEADS = 8
N_KV_HEADS = 2
ATTN_W = 512
KV_W = 128
WINDOW = 128
ROPE_THETA = 10000.0
HG_HEADS = 4
HG_K = 64
HG_W = 256
GM_GROUPS = 4
GM_DIM = 64
GM_W = 256
GM_CHUNK = 128
IN_W = 2560
FFN_DIM = 2816
N_EXPERTS = 8
EXPERT_DIM = 3584

VMEM_LIMIT = 56 * 1024 * 1024
NEG = -1e30

TM = 256
HT = 128
HG_SUB = 2
HG_LEVELS = (8, 16, 32, 64)
AB = 128
FFN_TM = 640
FFN_FC = 1408
MOE_TILE = 256
MOE_CAP = 17 * MOE_TILE
MOE_ITEMS = 2 * SEQ // MOE_CAP + N_EXPERTS
MOE_GC = 512
MOE_GB = 1024
MOE_NG = SEQ // MOE_GB
MOE_FC = 512
MOE_NF = EXPERT_DIM // MOE_FC
MOE_CW = 256
MOE_CT = 128
MOE_CS = MOE_CW // MOE_CT + 1
MOE_VMEM_LIMIT = 60 * 1024 * 1024


def _params(sem, vmem=VMEM_LIMIT):
    return pltpu.CompilerParams(dimension_semantics=sem, vmem_limit_bytes=vmem)


def _block_ones(n, seg):
    i = np.arange(n)
    return (i[:, None] // seg == i[None, :] // seg).astype(np.float32)


def _ada_kernel(a_ref, w_ref, b_ref, o_ref):
    a = a_ref[...]
    s = (a * jax.nn.sigmoid(a)).astype(BF16)
    w = w_ref[0].astype(BF16)
    o_ref[0] = jnp.dot(s, w, preferred_element_type=F32) + b_ref[0]


def _ada(cc, w_ada, b_ada):
    tn = 1536
    return pl.pallas_call(
        _ada_kernel,
        grid=(DEPTH, 6 * D // tn),
        in_specs=[
            pl.BlockSpec((8, D), lambda l, n: (0, 0)),
            pl.BlockSpec((1, D, tn), lambda l, n: (l, 0, n)),
            pl.BlockSpec((1, 1, tn), lambda l, n: (l, 0, n)),
        ],
        out_specs=pl.BlockSpec((1, 8, tn), lambda l, n: (l, 0, n)),
        out_shape=jax.ShapeDtypeStruct((DEPTH, 8, 6 * D), F32),
        compiler_params=_params(("parallel", "parallel")),
        name="ada_mod",
    )(cc, w_ada, b_ada.reshape(DEPTH, 1, 6 * D))


def _gelu(x):
    return 0.5 * x * (1.0 + lax.erf(x * 0.7071067811865476))


def _load_rows(refs, split):
    if not split:
        return refs[0][...], refs[1:]
    return jnp.where(pl.program_id(0) == 0, refs[0][...], refs[1][...]), refs[2:]


def _row_specs(xs, width, off=0):
    if not isinstance(xs, tuple):
        return [pl.BlockSpec((TM, width), lambda i: (i + off, 0))], [xs]
    assert off == 0 and CTX == TM
    return [pl.BlockSpec((TM, width), lambda i: (0, 0)),
            pl.BlockSpec((TM, width), lambda i: (jnp.maximum(i - 1, 0), 0))], list(xs)


def _inproj_kernel(*refs, split):
    x, refs = _load_rows(refs, split)
    (mod_ref, w_ref, cs_ref, sn_ref, gain_ref, bq_ref, bg_ref, gng_ref, ws_ref, bs_ref,
     q_ref, k_ref, v_ref, ph_ref, gm_ref) = refs
    r = lax.rsqrt(jnp.mean(x * x, axis=-1, keepdims=True) + EPS)
    mod = mod_ref[0]
    h = (x * r) * (1.0 + mod[1:2]) + mod[0:1]
    p = jnp.dot(h.astype(BF16), w_ref[...], preferred_element_type=F32)

    qk = p[:, :ATTN_W + KV_W]
    ss = jnp.dot((qk * qk).astype(BF16), bq_ref[...], preferred_element_type=F32)
    qkn = qk * lax.rsqrt(ss * (1.0 / HEAD_DIM) + EPS) * gain_ref[...]
    lane = lax.broadcasted_iota(jnp.int32, (TM, 128), 1)
    first_half = (lane // 16) % 2 == 0
    cs = cs_ref[...]
    sn = sn_ref[...]
    rot = []
    for j in range(5):
        xs = qkn[:, 128 * j:128 * (j + 1)]
        partner = jnp.where(first_half, pltpu.roll(xs, 112, 1), pltpu.roll(xs, 16, 1))
        rot.append(xs * cs + partner * sn)
    q_ref[...] = jnp.concatenate(rot[:4], axis=1).astype(BF16)

    low = lane < 64

    def dup(t):
        sw = pltpu.roll(t, 64, 1)
        return jnp.concatenate([jnp.where(low, t, sw), jnp.where(low, sw, t)], axis=1)

    k_ref[...] = dup(rot[4]).astype(BF16)
    v_ref[...] = dup(p[:, ATTN_W + KV_W:ATTN_W + 2 * KV_W]).astype(BF16)

    ph_ref[...] = p[:, 768:2048]

    u = _gelu(p[:, 2048:2304])
    vv = _gelu(p[:, 2304:2560])
    ssv = jnp.dot((vv * vv).astype(BF16), bg_ref[...], preferred_element_type=F32)
    vn = vv * lax.rsqrt(ssv * (1.0 / GM_DIM) + EPS) * gng_ref[...]
    group = lax.broadcasted_iota(jnp.int32, (GM_CHUNK, GM_W), 1) // GM_DIM
    for c in range(TM // GM_CHUNK):
        vc = vn[GM_CHUNK * c:GM_CHUNK * (c + 1)]
        acc = bs_ref[...]
        for g in range(GM_GROUPS):
            vg = jnp.where(group == g, vc, 0.0).astype(BF16)
            acc = acc + jnp.dot(ws_ref[g], vg, preferred_element_type=F32)
        gm_ref[GM_CHUNK * c:GM_CHUNK * (c + 1), :] = (u[GM_CHUNK * c:GM_CHUNK * (c + 1)] * acc).astype(BF16)


def _inproj(xs, modsel, w_in, cs, sn, gain, bq, bg, gng, ws, bs):
    nt = ROWS // TM
    const2 = lambda i: (0, 0)
    x_specs, x_args = _row_specs(xs, D)
    return pl.pallas_call(
        functools.partial(_inproj_kernel, split=isinstance(xs, tuple)),
        grid=(nt,),
        in_specs=x_specs + [
            pl.BlockSpec((1, 8, D), lambda i: (jnp.minimum(i, 1), 0, 0)),
            pl.BlockSpec((D, IN_W), const2),
            pl.BlockSpec((TM, 128), lambda i: (i, 0)),
            pl.BlockSpec((TM, 128), lambda i: (i, 0)),
            pl.BlockSpec((1, ATTN_W + KV_W), const2),
            pl.BlockSpec((ATTN_W + KV_W, ATTN_W + KV_W), const2),
            pl.BlockSpec((GM_W, GM_W), const2),
            pl.BlockSpec((1, GM_W), const2),
            pl.BlockSpec((GM_GROUPS, GM_CHUNK, GM_CHUNK), lambda i: (0, 0, 0)),
            pl.BlockSpec((GM_CHUNK, GM_W), const2),
        ],
        out_specs=[
            pl.BlockSpec((TM, ATTN_W), lambda i: (i, 0)),
            pl.BlockSpec((TM, 2 * KV_W), lambda i: (i, 0)),
            pl.BlockSpec((TM, 2 * KV_W), lambda i: (i, 0)),
            pl.BlockSpec((TM, 5 * HG_W), lambda i: (i, 0)),
            pl.BlockSpec((TM, GM_W), lambda i: (i, 0)),
        ],
        out_shape=[
            jax.ShapeDtypeStruct((ROWS, ATTN_W), BF16),
            jax.ShapeDtypeStruct((ROWS, 2 * KV_W), BF16),
            jax.ShapeDtypeStruct((ROWS, 2 * KV_W), BF16),
            jax.ShapeDtypeStruct((ROWS, 5 * HG_W), F32),
            jax.ShapeDtypeStruct((ROWS, GM_W), BF16),
        ],
        compiler_params=_params(("parallel",)),
        name="inproj",
    )(*x_args, modsel, w_in, cs, sn, gain, bq, bg, gng, ws, bs)


def _hgrn_consts():
    t = np.arange(HT)
    tt, ss = t[:, None], t[None, :]
    stack, masks = [], []
    for m in HG_LEVELS:
        same = (tt // m) == (ss // m)
        stack += [same & (ss <= tt), same & (ss > tt)]
        masks.append(((tt // (2 * m)) == (ss // (2 * m))) & ((tt % (2 * m)) >= m) & ((ss % (2 * m)) < m))
    stack.append(ss <= tt)
    masks.append(((tt // 8) == (ss // 8)) & (ss <= tt))
    lst_f = np.concatenate(stack, axis=0).astype(np.float32)
    lst_b = np.concatenate([b.T for b in stack], axis=0).astype(np.float32)
    msk_f = np.stack([np.tile(m, (1, HG_HEADS)) for m in masks]).astype(np.float32)
    msk_b = np.stack([np.tile(m.T, (1, HG_HEADS)) for m in masks]).astype(np.float32)
    e = np.zeros((8, HG_W, 128), np.float32)
    tl = np.zeros((128, HG_HEADS * HT), np.float32)
    for h in range(HG_HEADS):
        for j in range(8):
            e[j, h * HG_K:(h + 1) * HG_K, h * 8 + j] = 1.0
            tl[h * 8 + j, h * HT + t[(t % 8) == j]] = 1.0
    return lst_f, lst_b, msk_f, msk_b, e.reshape(8 * HG_W, 128), tl, _block_ones(HG_W, HG_K)


def _hgrn_direction(zq, zf, vi, lb, lst_ref, msk_ref, e_ref, tl_ref, bd_ref, st_ref, q3_ref, k3_ref, p3_ref):
    q = (zq * jax.nn.sigmoid(zq)) * (HG_K ** -0.5)
    f = lb + (1.0 - lb) * jax.nn.sigmoid(zf)
    g = jnp.log(f)
    kk = 1.0 - f
    g_hi = g.astype(BF16)
    g_lo = (g - g_hi.astype(F32)).astype(BF16)
    cum2 = jnp.dot(lst_ref[...], jnp.concatenate([g_hi, g_lo], axis=1), preferred_element_type=F32)
    cum = cum2[:, :HG_W] + cum2[:, HG_W:]
    nl = len(HG_LEVELS)
    c = cum[2 * nl * HT:(2 * nl + 1) * HT]
    ctot = jnp.sum(g, axis=0, keepdims=True)
    head = lax.broadcasted_iota(jnp.int32, (HT, HG_W), 1) // HG_K
    nt_dims = (((1,), (1,)), ((), ()))

    def per_head_rows(t):
        return jnp.concatenate([jnp.where(head == h, t, 0.0).astype(BF16) for h in range(HG_HEADS)], axis=0)

    yield
    st = st_ref[...]
    qe = (q * jnp.exp(c)).astype(BF16)
    o = lax.dot_general(qe, st.astype(BF16), nt_dims, preferred_element_type=F32)
    ke = (kk * jnp.exp(ctot - c)).astype(BF16)
    upd = lax.dot_general(vi.astype(BF16), ke, (((0,), (0,)), ((), ())), preferred_element_type=F32)
    st_ref[...] = st * jnp.exp(ctot) + upd * bd_ref[...]
    yield

    p8 = cum[0:HT]
    q3_ref[...] = q.reshape(HT // 8, 8, HG_W)
    k3_ref[...] = kk.reshape(HT // 8, 8, HG_W)
    p3_ref[...] = p8.reshape(HT // 8, 8, HG_W)
    slabs = []
    for j in range(8):
        kj = k3_ref[:, j:j + 1, :]
        pj = p3_ref[:, j:j + 1, :]
        slab = q3_ref[...] * kj * jnp.exp(jnp.minimum(p3_ref[...] - pj, 0.0))
        slabs.append(slab.reshape(HT, HG_W).astype(BF16))
        if j % 2 == 1:
            yield
    a = jnp.dot(jnp.concatenate(slabs, axis=1), e_ref[...], preferred_element_type=F32)
    sc = msk_ref[nl] * jnp.dot(a.astype(BF16), tl_ref[...], preferred_element_type=F32)

    for li in range(nl):
        yield
        pm = cum[2 * li * HT:(2 * li + 1) * HT]
        rm = cum[(2 * li + 1) * HT:(2 * li + 2) * HT]
        qt = (q * jnp.exp(pm)).astype(BF16)
        kt = per_head_rows(kk * jnp.exp(rm))
        sc = sc + msk_ref[li] * lax.dot_general(qt, kt, nt_dims, preferred_element_type=F32)

    yield
    return o + jnp.dot(sc.astype(BF16), per_head_rows(vi), preferred_element_type=F32)


def _hgrn_kernel(qf_ref, ff_ref, if_ref, qb_ref, fb_ref, ib_ref, lbf_ref, lbb_ref,
                 lstf_ref, lstb_ref, mskf_ref, mskb_ref, e_ref, tl_ref, bd_ref,
                 of_ref, ob_ref, stf_ref, stb_ref, *scr):
    @pl.when(pl.program_id(0) == 0)
    def _():
        stf_ref[...] = jnp.zeros_like(stf_ref)
        stb_ref[...] = jnp.zeros_like(stb_ref)

    scans, dests = [], []
    for k in range(HG_SUB):
        rf = slice(k * HT, (k + 1) * HT)
        rb = slice((HG_SUB - 1 - k) * HT, (HG_SUB - k) * HT)
        s3 = scr[6 * k:6 * (k + 1)]
        scans.append(_hgrn_direction(qf_ref[rf, :], ff_ref[rf, :], if_ref[rf, :], lbf_ref[...], lstf_ref, mskf_ref,
                                     e_ref, tl_ref, bd_ref, stf_ref, *s3[:3]))
        dests.append((of_ref, rf))
        scans.append(_hgrn_direction(qb_ref[rb, :], fb_ref[rb, :], ib_ref[rb, :], lbb_ref[...], lstb_ref, mskb_ref,
                                     e_ref, tl_ref, bd_ref, stb_ref, *s3[3:]))
        dests.append((ob_ref, rb))
    live = list(range(len(scans)))
    while live:
        for d in list(live):
            try:
                next(scans[d])
            except StopIteration as done:
                ref, rows = dests[d]
                ref[rows, :] = done.value
                live.remove(d)


def _hgrn(ph, lbf, lbb, consts):
    lst_f, lst_b, msk_f, msk_b, e, tl, bd = consts
    rows = HG_SUB * HT
    nt = ROWS // rows
    nc = CTX // rows

    def fwd(col):
        return pl.BlockSpec((rows, HG_W), lambda j: (j, col))

    def bwd(col):
        return pl.BlockSpec((rows, HG_W), lambda j: (jnp.where(j < nc, nc - 1 - j, nt + nc - 1 - j), col))

    def const(a):
        nd = a.ndim
        return pl.BlockSpec(a.shape, lambda j: (0,) * nd)

    scr3 = pltpu.VMEM((HT // 8, 8, HG_W), F32)
    return pl.pallas_call(
        _hgrn_kernel,
        grid=(nt,),
        in_specs=[fwd(0), fwd(1), fwd(3), bwd(0), bwd(2), bwd(3), const(lbf), const(lbb),
                  const(lst_f), const(lst_b), const(msk_f), const(msk_b), const(e), const(tl), const(bd)],
        out_specs=[pl.BlockSpec((rows, HG_W), lambda j: (j, 0)),
                   pl.BlockSpec((rows, HG_W), lambda j: (jnp.where(j < nc, nc - 1 - j, nt + nc - 1 - j), 0))],
        out_shape=[jax.ShapeDtypeStruct((ROWS, HG_W), F32), jax.ShapeDtypeStruct((ROWS, HG_W), F32)],
        scratch_shapes=[pltpu.VMEM((HG_W, HG_W), F32), pltpu.VMEM((HG_W, HG_W), F32)] + [scr3] * (6 * HG_SUB),
        compiler_params=_params(("arbitrary",)),
        name="hgrn_scan",
    )(ph, ph, ph, ph, ph, ph, lbf, lbb, lst_f, lst_b, msk_f, msk_b, e, tl, bd)


def _attn_core(q, kcat, vcat, bias, sink_ref, hm_ref, o_ref):
    nk = kcat.shape[0]
    nt_dims = (((1,), (1,)), ((), ()))
    pairs = []
    for g in range(N_KV_HEADS):
        kg = kcat[:, 128 * g:128 * (g + 1)]
        vg = vcat[:, 128 * g:128 * (g + 1)]
        vsel = (vg * hm_ref[0, :nk, :], vg * hm_ref[1, :nk, :])
        qs = []
        for pr in (2 * g, 2 * g + 1):
            qp = q[:, 128 * pr:128 * (pr + 1)]
            qs += [qp * hm_ref[0, :AB, :], qp * hm_ref[1, :AB, :]]
        s = lax.dot_general(jnp.concatenate(qs, axis=0), kg, nt_dims, preferred_element_type=F32)
        for pr in range(2):
            acc = None
            for a in range(2):
                hh = 2 * pr + a
                sk = sink_ref[4 * g + hh]
                sh = s[AB * hh:AB * (hh + 1)]
                if bias is not None:
                    sh = sh + bias
                m = jnp.maximum(jnp.max(sh, axis=1, keepdims=True), sk)
                e = jnp.exp(sh - m)
                den = jnp.sum(e, axis=1, keepdims=True) + jnp.exp(sk - m)
                pv = jnp.dot(e.astype(BF16), vsel[a], preferred_element_type=F32) * (1.0 / den)
                acc = pv if acc is None else acc + pv
            pairs.append(acc)
    o_ref[...] = jnp.concatenate(pairs, axis=1).astype(BF16)


def _attn_lat_kernel(q_ref, km_ref, k0_ref, kp_ref, kc_ref, vm_ref, v0_ref, vp_ref, vc_ref, sink_ref, hm_ref, o_ref):
    i = pl.program_id(0)
    nb = pl.num_programs(0)
    kcat = jnp.concatenate([km_ref[...], k0_ref[...], kp_ref[...], kc_ref[...]], axis=0)
    vcat = jnp.concatenate([vm_ref[...], v0_ref[...], vp_ref[...], vc_ref[...]], axis=0)
    nk = 3 * AB + CTX
    a = lax.broadcasted_iota(jnp.int32, (AB, nk), 0)
    r = lax.broadcasted_iota(jnp.int32, (AB, nk), 1)
    lo = jnp.where(i == 0, AB, 0)
    hi = jnp.where(i == nb - 1, 2 * AB - 1, 3 * AB - 1)
    lower = jnp.maximum(a + (AB - WINDOW), lo)
    upper = jnp.minimum(a + (AB + WINDOW), hi)
    in_band = jnp.where(r >= lower, jnp.where(r <= upper, 0.0, NEG), NEG)
    bias = jnp.where(r >= 3 * AB, 0.0, in_band)
    _attn_core(q_ref[...], kcat, vcat, bias, sink_ref, hm_ref, o_ref)


def _attn_ctx_kernel(q_ref, kc_ref, vc_ref, sink_ref, hm_ref, o_ref):
    _attn_core(q_ref[...], kc_ref[...], vc_ref[...], None, sink_ref, hm_ref, o_ref)


def _head_masks():
    hm = np.zeros((2, 3 * AB + CTX, 128), np.float32)
    hm[0, :, :64] = 1.0
    hm[1, :, 64:] = 1.0
    return hm


def _attn_lat(q, k, v, sink, hm):
    nb = SEQ // AB
    off = CTX // AB
    last = ROWS // AB - 1
    kv = lambda f: pl.BlockSpec((AB, 2 * KV_W), f)
    prev = lambda i: (i + off - 1, 0)
    cur = lambda i: (i + off, 0)
    nxt = lambda i: (jnp.minimum(i + off + 1, last), 0)
    ctx = pl.BlockSpec((CTX, 2 * KV_W), lambda i: (0, 0))
    return pl.pallas_call(
        _attn_lat_kernel,
        grid=(nb,),
        in_specs=[pl.BlockSpec((AB, ATTN_W), cur), kv(prev), kv(cur), kv(nxt), ctx,
                  kv(prev), kv(cur), kv(nxt), ctx,
                  pl.BlockSpec(memory_space=pltpu.SMEM),
                  pl.BlockSpec((2, 3 * AB + CTX, 128), lambda i: (0, 0, 0))],
        out_specs=pl.BlockSpec((AB, ATTN_W), lambda i: (i, 0)),
        out_shape=jax.ShapeDtypeStruct((SEQ, ATTN_W), BF16),
        compiler_params=_params(("parallel",)),
        name="attn_window",
    )(q, k, k, k, k, v, v, v, v, sink, hm)


def _attn_ctx(q, k, v, sink, hm):
    ctx = pl.BlockSpec((CTX, 2 * KV_W), lambda i: (0, 0))
    return pl.pallas_call(
        _attn_ctx_kernel,
        grid=(CTX // AB,),
        in_specs=[pl.BlockSpec((AB, ATTN_W), lambda i: (i, 0)), ctx, ctx,
                  pl.BlockSpec(memory_space=pltpu.SMEM),
                  pl.BlockSpec((2, 3 * AB + CTX, 128), lambda i: (0, 0, 0))],
        out_specs=pl.BlockSpec((AB, ATTN_W), lambda i: (i, 0)),
        out_shape=jax.ShapeDtypeStruct((CTX, ATTN_W), BF16),
        compiler_params=_params(("parallel",)),
        name="attn_context",
    )(q, k, v, sink, hm)


def _outproj_kernel(*refs, with_ctx, with_router, split):
    x_in, refs = _load_rows(refs, split)
    it = iter(refs)
    mod_ref = next(it)
    al_ref = next(it)
    ac_ref = next(it) if with_ctx else None
    of_ref, ob_ref, hg_ref, gm_ref, w_ref, hgain_ref, bh_ref = (next(it) for _ in range(7))
    rt_ref = next(it) if with_router else None
    lt_ref = next(it) if with_router else None
    x1_ref = next(it)
    h2_ref = next(it)
    if with_router:
        gate_ref, rank_ref, cnt_ref, base_ref = (next(it) for _ in range(4))

    mod = mod_ref[0]
    attn = al_ref[...]
    if with_ctx:
        attn = jnp.where(pl.program_id(0) == 0, ac_ref[...], attn)
    o = of_ref[...] + ob_ref[...]
    ss = jnp.dot((o * o).astype(BF16), bh_ref[...], preferred_element_type=F32)
    hg = hg_ref[...]
    y = o * lax.rsqrt(ss * (1.0 / HG_K) + EPS) * hgain_ref[...] * (hg * jax.nn.sigmoid(hg))
    mix = jnp.concatenate([attn, y.astype(BF16), gm_ref[...]], axis=1)
    proj = jnp.dot(mix, w_ref[...], preferred_element_type=F32)
    x1 = x_in + mod[2:3] * proj
    x1_ref[...] = x1
    r = lax.rsqrt(jnp.mean(x1 * x1, axis=-1, keepdims=True) + EPS)
    h2 = (x1 * r) * (1.0 + mod[4:5]) + mod[3:4]
    h2_hi = h2.astype(BF16)
    h2_ref[...] = h2_hi

    if with_router:
        h2_lo = (h2 - h2_hi.astype(F32)).astype(BF16)
        two = jnp.dot(h2_hi, rt_ref[...], preferred_element_type=F32)
        logits = two[:, :128] + two[:, 128:] + jnp.dot(h2_lo, rt_ref[:, :128], preferred_element_type=F32)
        lane = lax.broadcasted_iota(jnp.int32, logits.shape, 1)
        l1 = jnp.where(lane < N_EXPERTS, logits, NEG)
        m1 = jnp.max(l1, axis=1, keepdims=True)
        i1 = jnp.min(jnp.where(l1 == m1, lane, 128), axis=1, keepdims=True)
        l2 = jnp.where(lane == i1, NEG, l1)
        m2 = jnp.max(l2, axis=1, keepdims=True)
        i2 = jnp.min(jnp.where(l2 == m2, lane, 128), axis=1, keepdims=True)
        e2 = jnp.exp(m2 - m1)
        w1 = 1.0 / (1.0 + e2)
        gate_ref[...] = jnp.where(lane == i1, w1, 0.0) + jnp.where(lane == i2, e2 * w1, 0.0)

        @pl.when(pl.program_id(0) == 0)
        def _():
            base_ref[...] = jnp.zeros_like(base_ref)

        mem = jnp.where(lane == i1, 1.0, jnp.where(lane == i2, 1.0, 0.0))
        local = jnp.dot(lt_ref[...], mem.astype(BF16), preferred_element_type=F32)
        base = base_ref[...]
        rank_ref[...] = jnp.where(mem > 0.0, base + local, -1.0)
        after = base + jnp.sum(mem, axis=0, keepdims=True)
        cnt_ref[0, 0:1, :] = base
        cnt_ref[0, 1:2, :] = after
        cnt_ref[0, 2:8, :] = jnp.zeros((6, 128), F32)
        base_ref[...] = after


def _outproj(xs, modsel, attn_lat, attn_ctx, o_f, o_b, ph, gm, w_out, hgain, bh, router, *, with_ctx):
    with_router = router is not None
    off = 0 if with_ctx else CTX // TM
    nt = ROWS // TM - off
    rows_out = ROWS if with_ctx else SEQ
    row = lambda w: pl.BlockSpec((TM, w), lambda i: (i + off, 0))
    const2 = lambda i: (0, 0)
    x_specs, x_args = _row_specs(xs, D, off)
    in_specs = x_specs + [
        pl.BlockSpec((1, 8, D), lambda i: (jnp.minimum(i + off, 1), 0, 0)),
        pl.BlockSpec((TM, ATTN_W), lambda i: (jnp.maximum(i + off - CTX // TM, 0), 0))]
    args = x_args + [modsel, attn_lat]
    if with_ctx:
        in_specs.append(pl.BlockSpec((TM, ATTN_W), const2))
        args.append(attn_ctx)
    in_specs += [row(HG_W), row(HG_W),
                 pl.BlockSpec((TM, HG_W), lambda i: (i + off, 4)),
                 row(GM_W),
                 pl.BlockSpec((D, D), const2),
                 pl.BlockSpec((1, HG_W), const2),
                 pl.BlockSpec((HG_W, HG_W), const2)]
    args += [o_f, o_b, ph, gm, w_out, hgain, bh]
    out_specs = [pl.BlockSpec((TM, D), lambda i: (i, 0)), pl.BlockSpec((TM, D), lambda i: (i, 0))]
    out_shape = [jax.ShapeDtypeStruct((rows_out, D), F32), jax.ShapeDtypeStruct((rows_out, D), BF16)]
    scratch = []
    if with_router:
        t = np.arange(TM)
        strict_lower = jnp.asarray((t[None, :] < t[:, None]).astype(np.float32), BF16)
        in_specs += [pl.BlockSpec((D, 256), const2), pl.BlockSpec((TM, TM), const2)]
        args += [router, strict_lower]
        out_specs += [pl.BlockSpec((TM, 128), lambda i: (i, 0)), pl.BlockSpec((TM, 128), lambda i: (i, 0)),
                      pl.BlockSpec((1, 8, 128), lambda i: (i, 0, 0))]
        out_shape += [jax.ShapeDtypeStruct((rows_out, 128), F32), jax.ShapeDtypeStruct((rows_out, 128), F32),
                      jax.ShapeDtypeStruct((nt, 8, 128), F32)]
        scratch = [pltpu.VMEM((1, 128), F32)]
    return pl.pallas_call(
        functools.partial(_outproj_kernel, with_ctx=with_ctx, with_router=with_router, split=isinstance(xs, tuple)),
        grid=(nt,),
        in_specs=in_specs,
        out_specs=out_specs,
        out_shape=out_shape,
        scratch_shapes=scratch,
        compiler_params=_params(("arbitrary",) if with_router else ("parallel",)),
        name="outproj",
    )(*args)


def _ffn_kernel(h_ref, x_ref, mod_ref, wg_ref, wu_ref, wd_ref, o_ref):
    h = h_ref[...]
    acc = None
    for f in range(FFN_DIM // FFN_FC):
        cols = slice(f * FFN_FC, (f + 1) * FFN_FC)
        a = jnp.dot(h, wg_ref[:, cols], preferred_element_type=F32)
        b = jnp.dot(h, wu_ref[:, cols], preferred_element_type=F32)
        t = (a * jax.nn.sigmoid(a) * b).astype(BF16)
        y = jnp.dot(t, wd_ref[cols, :], preferred_element_type=F32)
        acc = y if acc is None else acc + y
    row = pl.program_id(0) * FFN_TM + lax.broadcasted_iota(jnp.int32, (FFN_TM, D), 0)
    gate = jnp.where(row < CTX, mod_ref[0][5:6], mod_ref[1][5:6])
    o_ref[...] = x_ref[...] + gate * acc


def _ffn(h2, x1, modsel, wg, wu, wd):
    nt = ROWS // FFN_TM
    resident = dict(pipeline_mode=pl.Buffered(1))
    return pl.pallas_call(
        _ffn_kernel,
        grid=(nt,),
        in_specs=[pl.BlockSpec((FFN_TM, D), lambda i: (i, 0)),
                  pl.BlockSpec((FFN_TM, D), lambda i: (i, 0)),
                  pl.BlockSpec((2, 8, D), lambda i: (0, 0, 0)),
                  pl.BlockSpec((D, FFN_DIM), lambda i: (0, 0), **resident),
                  pl.BlockSpec((D, FFN_DIM), lambda i: (0, 0), **resident),
                  pl.BlockSpec((FFN_DIM, D), lambda i: (0, 0), **resident)],
        out_specs=pl.BlockSpec((FFN_TM, D), lambda i: (i, 0)),
        out_shape=jax.ShapeDtypeStruct((ROWS, D), F32),
        compiler_params=_params(("parallel",)),
        name="ffn_dense",
    )(h2, x1, modsel, wg, wu, wd)


def _moe_index(i, s, ie, ibase, irows, cnt):
    valid = irows[i] > 0
    chunk = jnp.where(valid, jnp.minimum(s, MOE_NG - 1), MOE_NG - 1)
    fidx = jnp.where(valid, jnp.maximum(s - MOE_NG, 0), MOE_NF - 1)
    return chunk, fidx


def _moe_expert_kernel(ie_ref, ibase_ref, irows_ref, cnt_ref,
                       h_ref, rk_ref, wg_ref, wu_ref, wd_ref, ys_ref, yg_ref):
    i = pl.program_id(0)
    s = pl.program_id(1)
    e = ie_ref[i]
    base = ibase_ref[i]
    rows = irows_ref[i]
    ntile = (rows + MOE_TILE - 1) // MOE_TILE

    @pl.when(s == 0)
    def _():
        ys_ref[...] = jnp.zeros_like(ys_ref)

    def gather_chunk(j):
        c = s * (MOE_GB // MOE_GC) + j
        cols = slice(j * MOE_GC, (j + 1) * MOE_GC)
        lo = jnp.maximum(cnt_ref[c * N_EXPERTS + e] - base, 0)
        hi = jnp.minimum(cnt_ref[(c + 1) * N_EXPERTS + e] - base, rows) - 1
        a_lo = lo // MOE_TILE
        a_hi = jnp.where(hi >= lo, hi // MOE_TILE, a_lo - 1)
        rel = rk_ref[pl.ds(e, 1), cols].astype(jnp.int32) - base
        h = h_ref[cols, :]

        def tile_body(a, carry):
            off = pl.multiple_of(a * MOE_TILE, MOE_TILE)
            slot = lax.broadcasted_iota(jnp.int32, (MOE_TILE, MOE_GC), 0) + off
            onehot = jnp.where(rel == slot, 1.0, 0.0).astype(BF16)
            packed = jnp.dot(onehot, h, preferred_element_type=F32)
            ys_ref[pl.ds(off, MOE_TILE), :] = (ys_ref[pl.ds(off, MOE_TILE), :].astype(F32) + packed).astype(BF16)
            return carry

        lax.fori_loop(a_lo, a_hi + 1, tile_body, 0)

    @pl.when((s < MOE_NG) & (rows > 0))
    def _():
        for j in range(MOE_GB // MOE_GC):
            gather_chunk(j)

    def expert_step(mode):
        wg = wg_ref[0, 0].astype(BF16)
        wu = wu_ref[0, 0].astype(BF16)
        wd = wd_ref[0, 0].astype(BF16)

        def tile_body(k, carry):
            off = pl.multiple_of(k * MOE_TILE, MOE_TILE)
            xg = ys_ref[pl.ds(off, MOE_TILE), :]
            a = jnp.dot(xg, wg, preferred_element_type=F32)
            b = jnp.dot(xg, wu, preferred_element_type=F32)
            y = jnp.dot((a * jax.nn.sigmoid(a) * b).astype(BF16), wd, preferred_element_type=F32)
            if mode == "first":
                yg_ref[pl.ds(off, MOE_TILE), :] = y
            elif mode == "middle":
                yg_ref[pl.ds(off, MOE_TILE), :] += y
            else:
                ys_ref[pl.ds(off, MOE_TILE), :] = (yg_ref[pl.ds(off, MOE_TILE), :] + y).astype(BF16)
            return carry

        lax.fori_loop(0, ntile, tile_body, 0)

    f = s - MOE_NG
    pl.when((f == 0) & (rows > 0))(functools.partial(expert_step, "first"))
    pl.when((f > 0) & (f < MOE_NF - 1) & (rows > 0))(functools.partial(expert_step, "middle"))
    pl.when((f == MOE_NF - 1) & (rows > 0))(functools.partial(expert_step, "last"))


def _moe_experts(tables, h2, rk_t, wg, wu, wd):
    ie, ibase, irows, cnt = tables

    def hmap(i, s, ie, ibase, irows, cnt):
        return (_moe_index(i, s, ie, ibase, irows, cnt)[0], 0)

    def rmap(i, s, ie, ibase, irows, cnt):
        return (0, _moe_index(i, s, ie, ibase, irows, cnt)[0])

    def upmap(i, s, ie, ibase, irows, cnt):
        return (0, ie[i], 0, _moe_index(i, s, ie, ibase, irows, cnt)[1])

    def downmap(i, s, ie, ibase, irows, cnt):
        return (0, ie[i], _moe_index(i, s, ie, ibase, irows, cnt)[1], 0)

    grid_spec = pltpu.PrefetchScalarGridSpec(
        num_scalar_prefetch=4,
        grid=(MOE_ITEMS, MOE_NG + MOE_NF),
        in_specs=[pl.BlockSpec((MOE_GB, D), hmap),
                  pl.BlockSpec((N_EXPERTS, MOE_GB), rmap),
                  pl.BlockSpec((1, 1, D, MOE_FC), upmap),
                  pl.BlockSpec((1, 1, D, MOE_FC), upmap),
                  pl.BlockSpec((1, 1, MOE_FC, D), downmap)],
        out_specs=pl.BlockSpec((MOE_CAP, D), lambda i, s, ie, ibase, irows, cnt: (i, 0)),
        scratch_shapes=[pltpu.VMEM((MOE_CAP, D), F32)],
    )
    return pl.pallas_call(
        _moe_expert_kernel,
        grid_spec=grid_spec,
        out_shape=jax.ShapeDtypeStruct((MOE_ITEMS * MOE_CAP, D), BF16),
        compiler_params=_params(("arbitrary", "arbitrary"), vmem=MOE_VMEM_LIMIT),
        name="moe_experts",
    )(ie, ibase, irows, cnt, h2, rk_t, wg, wu, wd)


def _moe_combine_kernel(tix_ref, se_ref, f0_ref, f1_ref, f2_ref, wide_ref, x_ref, gate_ref, rk_ref, gf_ref, *refs):
    y_refs, o_ref = refs[:-1], refs[-1]
    w = pl.program_id(0)

    def unpack(ntile):
        gate = gate_ref[...]
        rk = rk_ref[...]
        lane = lax.broadcasted_iota(jnp.int32, gate.shape, 1)
        slot = lax.broadcasted_iota(jnp.int32, (MOE_CW, ntile * MOE_CT), 1)
        acc = jnp.zeros((MOE_CW, D), F32)
        for e in range(N_EXPERTS):
            rcol = jnp.sum(jnp.where(lane == e, rk, 0.0), axis=1, keepdims=True)
            gcol = jnp.sum(jnp.where(lane == e, gate, 0.0), axis=1, keepdims=True)
            pos = rcol.astype(jnp.int32) + (se_ref[e] - tix_ref[w * N_EXPERTS + e] * MOE_CT)
            pos = jnp.where(rcol < 0.0, -1, pos)
            onehot = jnp.where(pos == slot, 1.0, 0.0).astype(BF16)
            ycat = jnp.concatenate([y_refs[MOE_CS * e + j][...] for j in range(ntile)], axis=0)
            acc = acc + gcol * jnp.dot(onehot, ycat, preferred_element_type=F32)
        o_ref[...] = x_ref[...] + gf_ref[...] * acc

    pl.when(wide_ref[w] == 0)(functools.partial(unpack, MOE_CS - 1))
    pl.when(wide_ref[w] != 0)(functools.partial(unpack, MOE_CS))


def _moe_combine(ctab, x1, gate, rk, gf, ys):
    assert MOE_CS == 3
    row = lambda wd_: pl.BlockSpec((MOE_CW, wd_), lambda w, *tabs: (w, 0))

    def ymap(e, j):
        return lambda w, *tabs: (tabs[2 + j][w * N_EXPERTS + e], 0)

    y_specs = [pl.BlockSpec((MOE_CT, D), ymap(e, j)) for e in range(N_EXPERTS) for j in range(MOE_CS)]
    grid_spec = pltpu.PrefetchScalarGridSpec(
        num_scalar_prefetch=len(ctab),
        grid=(SEQ // MOE_CW,),
        in_specs=[row(D), row(128), row(128), pl.BlockSpec((1, D), lambda w, *tabs: (0, 0))] + y_specs,
        out_specs=row(D),
    )
    return pl.pallas_call(
        _moe_combine_kernel,
        grid_spec=grid_spec,
        out_shape=jax.ShapeDtypeStruct((SEQ, D), F32),
        compiler_params=_params(("arbitrary",)),
        name="moe_combine",
    )(*ctab, x1, gate, rk, gf, *([ys] * (N_EXPERTS * MOE_CS)))


def _moe_tables(cnt):
    before = cnt[:, 0, :N_EXPERTS].astype(jnp.int32)
    total = cnt[-1, 1, :N_EXPERTS].astype(jnp.int32)
    nit = (total + MOE_CAP - 1) // MOE_CAP
    ends = jnp.cumsum(nit)
    first = ends - nit
    item = jnp.arange(MOE_ITEMS, dtype=jnp.int32)
    n_items = ends[-1]
    ie = jnp.minimum(jnp.sum((ends[None, :] <= item[:, None]).astype(jnp.int32), axis=1), N_EXPERTS - 1)
    last_e = ie[jnp.maximum(n_items - 1, 0)]
    valid = item < n_items
    ibase = jnp.where(valid, (item - first[ie]) * MOE_CAP, 0)
    irows = jnp.where(valid, jnp.clip(total[ie] - ibase, 0, MOE_CAP), 0)
    ie = jnp.where(valid, ie, last_e)
    per_chunk = MOE_GC // TM
    cnt_g = jnp.concatenate([before[::per_chunk], total[None, :]], axis=0).reshape(-1)
    se = first * MOE_CAP
    per_win = MOE_CW // TM
    win_before = before[::per_win]
    win_rows = jnp.concatenate([win_before[1:], total[None, :]], axis=0) - win_before
    start = se[None, :] + win_before
    tix = jnp.minimum(start // MOE_CT, n_items * (MOE_CAP // MOE_CT) - MOE_CS)
    tix = jnp.maximum(tix, 0)
    end = start - tix * MOE_CT + win_rows
    nw = tix.shape[0]
    earlier = (jnp.arange(nw)[None, :] <= jnp.arange(nw)[:, None])[:, :, None]
    fetch = []
    for j in range(MOE_CS):
        want = jnp.where(end > j * MOE_CT, tix + j, 0)
        fetch.append(jnp.max(jnp.where(earlier, want[None, :, :], 0), axis=1).reshape(-1).astype(jnp.int32))
    etab = (ie, ibase.astype(jnp.int32), irows.astype(jnp.int32), cnt_g)
    wide = jnp.max((end > (MOE_CS - 1) * MOE_CT).astype(jnp.int32), axis=1)
    ctab = (tix.reshape(-1).astype(jnp.int32), se.astype(jnp.int32)) + tuple(fetch) + (wide,)
    return etab, ctab


def _rope_tables():
    rows = SEQ // GRID_W
    row = np.repeat(np.arange(rows, dtype=np.float32), GRID_W)
    col = np.tile(np.arange(GRID_W, dtype=np.float32), rows)
    axis_dim = HEAD_DIM // 2
    inv_freq = (np.float32(ROPE_THETA) ** (-np.arange(0, axis_dim, 2, dtype=np.float32) / np.float32(axis_dim)))
    ang_r = row[:, None] * inv_freq.astype(np.float32)
    ang_c = col[:, None] * inv_freq.astype(np.float32)
    cos_r, sin_r, cos_c, sin_c = np.cos(ang_r), np.sin(ang_r), np.cos(ang_c), np.sin(ang_c)
    cs = np.concatenate([cos_r, cos_r, cos_c, cos_c], axis=1)
    sn = np.concatenate([-sin_r, sin_r, -sin_c, sin_c], axis=1)
    cs = np.concatenate([np.ones((CTX, HEAD_DIM), np.float32), cs], axis=0)
    sn = np.concatenate([np.zeros((CTX, HEAD_DIM), np.float32), sn], axis=0)
    return (jnp.asarray(np.tile(cs, (1, 2)).astype(np.float32)), jnp.asarray(np.tile(sn, (1, 2)).astype(np.float32)))


def kernel(x, c, ctx, c_ctx, w_ada, b_ada, w_in, w_out, q_norm_gain, k_norm_gain, attn_sink, hgrn_lower_bound, hgrn_out_gain, gmlp_w_s, gmlp_b_s, gmlp_norm_gain, ffn_w_gate, ffn_w_up, ffn_w_down, moe_router, moe_w_gate, moe_w_up, moe_w_down):
    cc = jnp.concatenate([c, c_ctx[None, :], jnp.zeros((6, D), F32)], axis=0)
    mod = _ada(cc, w_ada, b_ada)
    mod6 = mod[:, :2].reshape(DEPTH, 2, 6, D)
    modsel = jnp.pad(jnp.stack([mod6[:, 1], mod6[:, 0]], axis=1), ((0, 0), (0, 0), (0, 2), (0, 0)))

    lb_soft = jax.nn.softmax(hgrn_lower_bound.astype(F32), axis=1)
    lower = jnp.cumsum(lb_soft, axis=1) - lb_soft[:, :1]

    cs, sn = _rope_tables()
    hconsts = _hgrn_consts()
    hconsts = tuple(jnp.asarray(a, BF16) for a in hconsts[:2]) + tuple(jnp.asarray(a, F32) for a in hconsts[2:4]) \
        + tuple(jnp.asarray(a, BF16) for a in hconsts[4:6]) + (jnp.asarray(hconsts[6], F32),)
    bq = jnp.asarray(_block_ones(ATTN_W + KV_W, HEAD_DIM), BF16)
    bg = jnp.asarray(_block_ones(GM_W, GM_DIM), BF16)
    hm = jnp.asarray(_head_masks(), BF16)

    xs = (ctx[0], x[0])
    for layer in range(DEPTH):
        last = layer == DEPTH - 1
        gain = jnp.concatenate([jnp.tile(q_norm_gain[layer], N_Q_HEADS) * (HEAD_DIM ** -0.5),
                                jnp.tile(k_norm_gain[layer], N_KV_HEADS)])[None, :]
        bias = jnp.repeat(gmlp_b_s[layer].T, GM_DIM, axis=1)
        q, k, v, ph, gm = _inproj(xs, modsel[layer], w_in[layer].astype(BF16), cs, sn, gain, bq, bg,
                                  gmlp_norm_gain[layer][None, :], gmlp_w_s[layer].astype(BF16), bias)
        o_f, o_b = _hgrn(ph, lower[0, layer][None, :], lower[1, layer][None, :], hconsts)
        sink = attn_sink[layer]
        attn_lat = _attn_lat(q, k, v, sink, hm)
        hgain = jnp.tile(hgrn_out_gain[layer], HG_HEADS)[None, :]
        w_o = w_out[layer].astype(BF16)
        if not last:
            attn_ctx = _attn_ctx(q, k, v, sink, hm)
            x1, h2 = _outproj(xs, modsel[layer], attn_lat, attn_ctx, o_f, o_b, ph, gm, w_o, hgain, bg, None,
                              with_ctx=True)
        else:
            r32 = jnp.pad(moe_router[layer // 2], ((0, 0), (0, 128 - N_EXPERTS)))
            r_hi = r32.astype(BF16)
            router = jnp.concatenate([r_hi, (r32 - r_hi.astype(F32)).astype(BF16)], axis=1)
            x1, h2, gate, rk, cnt = _outproj(xs, modsel[layer], attn_lat, None, o_f, o_b, ph, gm, w_o, hgain, bg,
                                             router, with_ctx=False)
        i = layer // 2
        if layer % 2 == 0:
            xs = _ffn(h2, x1, modsel[layer], ffn_w_gate[i].astype(BF16), ffn_w_up[i].astype(BF16),
                      ffn_w_down[i].astype(BF16))
        else:
            etab, ctab = _moe_tables(cnt)
            ys = _moe_experts(etab, h2, rk[:, :N_EXPERTS].T, moe_w_gate[i:i + 1], moe_w_up[i:i + 1],
                              moe_w_down[i:i + 1])
            xs = _moe_combine(ctab, x1, gate, rk, modsel[layer, 1, 5][None, :], ys)
    return xs[None]
```

```python
import functools

import numpy as np
import jax
import jax.numpy as jnp
from jax import lax
from jax.experimental import pallas as pl
from jax.experimental.pallas import tpu as pltpu

F32 = jnp.float32
BF16 = jnp.bfloat16

D = 1024
SEQ = 16384
CTX = 256
ROWS = CTX + SEQ
DEPTH = 2
GRID_W = 64
EPS = 1e-6
HEAD_DIM = 64
N_Q_HEADS = 8
N_KV_HEADS = 2
ATTN_W = 512
KV_W = 128
WINDOW = 128
ROPE_THETA = 10000.0
HG_HEADS = 4
HG_K = 64
HG_W = 256
GM_GROUPS = 4
GM_DIM = 64
GM_W = 256
GM_CHUNK = 128
IN_W = 2560
FFN_DIM = 2816
N_EXPERTS = 8
EXPERT_DIM = 3584

VMEM_LIMIT = 56 * 1024 * 1024
NEG = -1e30
LOG2E = 1.4426950408889634

TM = 256
HT = 128
HG_SUB = 2
HG_LEVELS = (8, 16, 32, 64)
AB = 128
FFN_TM = 640
FFN_FC = 1408
MOE_TILE = 256
MOE_CAP = 17 * MOE_TILE
MOE_ITEMS = 2 * SEQ // MOE_CAP + N_EXPERTS
MOE_GC = 256
MOE_GT = 128
MOE_GB = 1024
MOE_NG = SEQ // MOE_GB
MOE_FC = 512
MOE_NF = EXPERT_DIM // MOE_FC
MOE_CW = 256
MOE_CT = 128
MOE_CS = MOE_CW // MOE_CT + 1
MOE_VMEM_LIMIT = 60 * 1024 * 1024


def _params(sem, vmem=VMEM_LIMIT):
    return pltpu.CompilerParams(dimension_semantics=sem, vmem_limit_bytes=vmem)


def _block_ones(n, seg):
    i = np.arange(n)
    return (i[:, None] // seg == i[None, :] // seg).astype(np.float32)


def _ada_kernel(a_ref, w_ref, b_ref, o_ref):
    a = a_ref[...]
    s = (a * jax.nn.sigmoid(a)).astype(BF16)
    w = w_ref[0].astype(BF16)
    o_ref[0] = jnp.dot(s, w, preferred_element_type=F32) + b_ref[0]


def _ada(cc, w_ada, b_ada):
    tn = 1536
    return pl.pallas_call(
        _ada_kernel,
        grid=(DEPTH, 6 * D // tn),
        in_specs=[
            pl.BlockSpec((8, D), lambda l, n: (0, 0)),
            pl.BlockSpec((1, D, tn), lambda l, n: (l, 0, n)),
            pl.BlockSpec((1, 1, tn), lambda l, n: (l, 0, n)),
        ],
        out_specs=pl.BlockSpec((1, 8, tn), lambda l, n: (l, 0, n)),
        out_shape=jax.ShapeDtypeStruct((DEPTH, 8, 6 * D), F32),
        compiler_params=_params(("parallel", "parallel")),
        name="ada_mod",
    )(cc, w_ada, b_ada.reshape(DEPTH, 1, 6 * D))


def _gelu(x):
    return 0.5 * x * (1.0 + lax.erf(x * 0.7071067811865476))


def _load_rows(refs, split):
    if not split:
        return refs[0][...], refs[1:]
    return jnp.where(pl.program_id(0) == 0, refs[0][...], refs[1][...]), refs[2:]


def _row_specs(xs, width, off=0):
    if not isinstance(xs, tuple):
        return [pl.BlockSpec((TM, width), lambda i: (i + off, 0))], [xs]
    assert off == 0 and CTX == TM
    return [pl.BlockSpec((TM, width), lambda i: (0, 0)),
            pl.BlockSpec((TM, width), lambda i: (jnp.maximum(i - 1, 0), 0))], list(xs)


def _inproj_kernel(*refs, split):
    x, refs = _load_rows(refs, split)
    (mod_ref, w_ref, cs_ref, sn_ref, gain_ref, bq_ref, bg_ref, gng_ref, ws_ref, bs_ref,
     q_ref, k_ref, v_ref, ph_ref, gm_ref) = refs
    r = lax.rsqrt(jnp.mean(x * x, axis=-1, keepdims=True) + EPS)
    mod = mod_ref[0]
    h = (x * r) * (1.0 + mod[1:2]) + mod[0:1]
    p = jnp.dot(h.astype(BF16), w_ref[...], preferred_element_type=F32)

    qk = p[:, :ATTN_W + KV_W]
    ss = jnp.dot((qk * qk).astype(BF16), bq_ref[...], preferred_element_type=F32)
    qkn = qk * lax.rsqrt(ss * (1.0 / HEAD_DIM) + EPS) * gain_ref[...]
    lane = lax.broadcasted_iota(jnp.int32, (TM, 128), 1)
    first_half = (lane // 16) % 2 == 0
    cs = cs_ref[...]
    sn = sn_ref[...]
    rot = []
    for j in range(5):
        xs = qkn[:, 128 * j:128 * (j + 1)]
        partner = jnp.where(first_half, pltpu.roll(xs, 112, 1), pltpu.roll(xs, 16, 1))
        rot.append(xs * cs + partner * sn)
    q_ref[...] = jnp.concatenate(rot[:4], axis=1).astype(BF16)

    low = lane < 64

    def dup(t):
        sw = pltpu.roll(t, 64, 1)
        return jnp.concatenate([jnp.where(low, t, sw), jnp.where(low, sw, t)], axis=1)

    k_ref[...] = dup(rot[4]).astype(BF16)
    v_ref[...] = dup(p[:, ATTN_W + KV_W:ATTN_W + 2 * KV_W]).astype(BF16)

    ph_ref[...] = p[:, 768:2048]

    u = _gelu(p[:, 2048:2304])
    vv = _gelu(p[:, 2304:2560])
    ssv = jnp.dot((vv * vv).astype(BF16), bg_ref[...], preferred_element_type=F32)
    vn = vv * lax.rsqrt(ssv * (1.0 / GM_DIM) + EPS) * gng_ref[...]
    group = lax.broadcasted_iota(jnp.int32, (GM_CHUNK, GM_W), 1) // GM_DIM
    for c in range(TM // GM_CHUNK):
        vc = vn[GM_CHUNK * c:GM_CHUNK * (c + 1)]
        acc = bs_ref[...]
        for g in range(GM_GROUPS):
            vg = jnp.where(group == g, vc, 0.0).astype(BF16)
            acc = acc + jnp.dot(ws_ref[g], vg, preferred_element_type=F32)
        gm_ref[GM_CHUNK * c:GM_CHUNK * (c + 1), :] = (u[GM_CHUNK * c:GM_CHUNK * (c + 1)] * acc).astype(BF16)


def _inproj(xs, modsel, w_in, cs, sn, gain, bq, bg, gng, ws, bs):
    nt = ROWS // TM
    const2 = lambda i: (0, 0)
    x_specs, x_args = _row_specs(xs, D)
    return pl.pallas_call(
        functools.partial(_inproj_kernel, split=isinstance(xs, tuple)),
        grid=(nt,),
        in_specs=x_specs + [
            pl.BlockSpec((1, 8, D), lambda i: (jnp.minimum(i, 1), 0, 0)),
            pl.BlockSpec((D, IN_W), const2),
            pl.BlockSpec((TM, 128), lambda i: (i, 0)),
            pl.BlockSpec((TM, 128), lambda i: (i, 0)),
            pl.BlockSpec((1, ATTN_W + KV_W), const2),
            pl.BlockSpec((ATTN_W + KV_W, ATTN_W + KV_W), const2),
            pl.BlockSpec((GM_W, GM_W), const2),
            pl.BlockSpec((1, GM_W), const2),
            pl.BlockSpec((GM_GROUPS, GM_CHUNK, GM_CHUNK), lambda i: (0, 0, 0)),
            pl.BlockSpec((GM_CHUNK, GM_W), const2),
        ],
        out_specs=[
            pl.BlockSpec((TM, ATTN_W), lambda i: (i, 0)),
            pl.BlockSpec((TM, 2 * KV_W), lambda i: (i, 0)),
            pl.BlockSpec((TM, 2 * KV_W), lambda i: (i, 0)),
            pl.BlockSpec((TM, 5 * HG_W), lambda i: (i, 0)),
            pl.BlockSpec((TM, GM_W), lambda i: (i, 0)),
        ],
        out_shape=[
            jax.ShapeDtypeStruct((ROWS, ATTN_W), BF16),
            jax.ShapeDtypeStruct((ROWS, 2 * KV_W), BF16),
            jax.ShapeDtypeStruct((ROWS, 2 * KV_W), BF16),
            jax.ShapeDtypeStruct((ROWS, 5 * HG_W), F32),
            jax.ShapeDtypeStruct((ROWS, GM_W), BF16),
        ],
        compiler_params=_params(("parallel",)),
        name="inproj",
    )(*x_args, modsel, w_in, cs, sn, gain, bq, bg, gng, ws, bs)


def _hgrn_consts():
    t = np.arange(HT)
    tt, ss = t[:, None], t[None, :]
    stack, masks = [], []
    for m in HG_LEVELS:
        same = (tt // m) == (ss // m)
        stack += [same & (ss <= tt), same & (ss > tt)]
        masks.append(((tt // (2 * m)) == (ss // (2 * m))) & ((tt % (2 * m)) >= m) & ((ss % (2 * m)) < m))
    stack.append(ss <= tt)
    masks.append(((tt // 8) == (ss // 8)) & (ss <= tt))
    lst_f = np.tile(np.concatenate(stack, axis=0).astype(np.float32), (1, 2))
    lst_b = np.tile(np.concatenate([b.T for b in stack], axis=0).astype(np.float32), (1, 2))
    msk_f = np.stack([np.tile(m, (1, HG_HEADS)) for m in masks]).astype(np.float32)
    msk_b = np.stack([np.tile(m.T, (1, HG_HEADS)) for m in masks]).astype(np.float32)
    e = np.zeros((8, HG_W, 128), np.float32)
    tl = np.zeros((128, HG_HEADS * HT), np.float32)
    for h in range(HG_HEADS):
        for j in range(8):
            e[j, h * HG_K:(h + 1) * HG_K, h * 8 + j] = 1.0
            tl[h * 8 + j, h * HT + t[(t % 8) == j]] = 1.0
    lane_head = np.arange(HG_W) // HG_K
    hm = np.stack([np.tile((lane_head == h).astype(np.float32), (HT, 1)) for h in range(HG_HEADS)])
    return lst_f, lst_b, msk_f, msk_b, e.reshape(8 * HG_W, 128), tl, _block_ones(HG_W, HG_K), hm


def _hgrn_direction(zq, zf, vi, lb, lst_ref, msk_ref, e_ref, tl_ref, bd_ref, hm_ref, st_ref, q3_ref, k3_ref, p3_ref):
    q = (zq * jax.nn.sigmoid(zq)) * (HG_K ** -0.5)
    f = lb + (1.0 - lb) * jax.nn.sigmoid(zf)
    g = jnp.log(f) * LOG2E
    kk = 1.0 - f
    g_hi = g.astype(BF16)
    g_lo = (g - g_hi.astype(F32)).astype(BF16)
    cum = jnp.dot(lst_ref[...], jnp.concatenate([g_hi, g_lo], axis=0), preferred_element_type=F32)
    nl = len(HG_LEVELS)
    c = cum[2 * nl * HT:(2 * nl + 1) * HT]
    ctot = jnp.sum(g, axis=0, keepdims=True)
    nt_dims = (((1,), (1,)), ((), ()))

    def per_head_rows(t):
        tb = t.astype(BF16)
        return jnp.concatenate([tb * hm_ref[h] for h in range(HG_HEADS)], axis=0)

    yield
    st = st_ref[...]
    qe = (q * jnp.exp2(c)).astype(BF16)
    o = lax.dot_general(qe, st.astype(BF16), nt_dims, preferred_element_type=F32)
    ke = (kk * jnp.exp2(ctot - c)).astype(BF16)
    upd = lax.dot_general(vi.astype(BF16), ke, (((0,), (0,)), ((), ())), preferred_element_type=F32)
    st_ref[...] = st * jnp.exp2(ctot) + upd * bd_ref[...]
    yield

    p8 = cum[0:HT]
    q3_ref[...] = q.reshape(HT // 8, 8, HG_W)
    k3_ref[...] = kk.reshape(HT // 8, 8, HG_W)
    p3_ref[...] = p8.reshape(HT // 8, 8, HG_W)
    slabs = []
    for j in range(8):
        kj = k3_ref[:, j:j + 1, :]
        pj = p3_ref[:, j:j + 1, :]
        slab = q3_ref[...] * kj * jnp.exp2(jnp.minimum(p3_ref[...] - pj, 0.0))
        slabs.append(slab.reshape(HT, HG_W).astype(BF16))
        if j % 2 == 1:
            yield
    a = jnp.dot(jnp.concatenate(slabs, axis=1), e_ref[...], preferred_element_type=F32)
    sc = msk_ref[nl] * jnp.dot(a.astype(BF16), tl_ref[...], preferred_element_type=F32)

    for li in range(nl):
        yield
        pm = cum[2 * li * HT:(2 * li + 1) * HT]
        rm = cum[(2 * li + 1) * HT:(2 * li + 2) * HT]
        qt = (q * jnp.exp2(pm)).astype(BF16)
        kt = per_head_rows(kk * jnp.exp2(rm))
        sc = sc + msk_ref[li] * lax.dot_general(qt, kt, nt_dims, preferred_element_type=F32)

    yield
    return o + jnp.dot(sc.astype(BF16), per_head_rows(vi), preferred_element_type=F32)


def _hgrn_kernel(qf_ref, ff_ref, if_ref, qb_ref, fb_ref, ib_ref, lbf_ref, lbb_ref,
                 lstf_ref, lstb_ref, mskf_ref, mskb_ref, e_ref, tl_ref, bd_ref, hm_ref,
                 of_ref, ob_ref, stf_ref, stb_ref, *scr):
    @pl.when(pl.program_id(0) == 0)
    def _():
        stf_ref[...] = jnp.zeros_like(stf_ref)
        stb_ref[...] = jnp.zeros_like(stb_ref)

    scans, dests = [], []
    for k in range(HG_SUB):
        rf = slice(k * HT, (k + 1) * HT)
        rb = slice((HG_SUB - 1 - k) * HT, (HG_SUB - k) * HT)
        s3 = scr[6 * k:6 * (k + 1)]
        scans.append(_hgrn_direction(qf_ref[rf, :], ff_ref[rf, :], if_ref[rf, :], lbf_ref[...], lstf_ref, mskf_ref,
                                     e_ref, tl_ref, bd_ref, hm_ref, stf_ref, *s3[:3]))
        dests.append((of_ref, rf))
        scans.append(_hgrn_direction(qb_ref[rb, :], fb_ref[rb, :], ib_ref[rb, :], lbb_ref[...], lstb_ref, mskb_ref,
                                     e_ref, tl_ref, bd_ref, hm_ref, stb_ref, *s3[3:]))
        dests.append((ob_ref, rb))
    live = list(range(len(scans)))
    while live:
        for d in list(live):
            try:
                next(scans[d])
            except StopIteration as done:
                ref, rows = dests[d]
                ref[rows, :] = done.value
                live.remove(d)


def _hgrn(ph, lbf, lbb, consts):
    lst_f, lst_b, msk_f, msk_b, e, tl, bd, hm = consts
    rows = HG_SUB * HT
    nt = ROWS // rows
    nc = CTX // rows

    def fwd(col):
        return pl.BlockSpec((rows, HG_W), lambda j: (j, col))

    def bwd(col):
        return pl.BlockSpec((rows, HG_W), lambda j: (jnp.where(j < nc, nc - 1 - j, nt + nc - 1 - j), col))

    def const(a):
        nd = a.ndim
        return pl.BlockSpec(a.shape, lambda j: (0,) * nd)

    scr3 = pltpu.VMEM((HT // 8, 8, HG_W), F32)
    return pl.pallas_call(
        _hgrn_kernel,
        grid=(nt,),
        in_specs=[fwd(0), fwd(1), fwd(3), bwd(0), bwd(2), bwd(3), const(lbf), const(lbb),
                  const(lst_f), const(lst_b), const(msk_f), const(msk_b), const(e), const(tl), const(bd), const(hm)],
        out_specs=[pl.BlockSpec((rows, HG_W), lambda j: (j, 0)),
                   pl.BlockSpec((rows, HG_W), lambda j: (jnp.where(j < nc, nc - 1 - j, nt + nc - 1 - j), 0))],
        out_shape=[jax.ShapeDtypeStruct((ROWS, HG_W), F32), jax.ShapeDtypeStruct((ROWS, HG_W), F32)],
        scratch_shapes=[pltpu.VMEM((HG_W, HG_W), F32), pltpu.VMEM((HG_W, HG_W), F32)] + [scr3] * (6 * HG_SUB),
        compiler_params=_params(("arbitrary",)),
        name="hgrn_scan",
    )(ph, ph, ph, ph, ph, ph, lbf, lbb, lst_f, lst_b, msk_f, msk_b, e, tl, bd, hm)


def _attn_core(q, kcat, vcat, bias, sink_ref, hm_ref, o_ref):
    nk = kcat.shape[0]
    nt_dims = (((1,), (1,)), ((), ()))
    low = lax.broadcasted_iota(jnp.int32, (AB, 128), 1) < HEAD_DIM
    den_lane = (HEAD_DIM, 0)
    scores, vsels = [], []
    for g in range(N_KV_HEADS):
        kg = kcat[:, 128 * g:128 * (g + 1)]
        vg = vcat[:, 128 * g:128 * (g + 1)]
        vsels.append((vg * hm_ref[0, :nk, :] + hm_ref[2, :nk, :], vg * hm_ref[1, :nk, :] + hm_ref[3, :nk, :]))
        qs = []
        for pr in (2 * g, 2 * g + 1):
            qp = q[:, 128 * pr:128 * (pr + 1)]
            qs += [qp * hm_ref[0, :AB, :], qp * hm_ref[1, :AB, :]]
        scores.append(lax.dot_general(jnp.concatenate(qs, axis=0), kg, nt_dims, preferred_element_type=F32))
    halves = {}
    for hh in range(N_Q_HEADS // N_KV_HEADS):
        for g in range(N_KV_HEADS):
            a = hh % 2
            sk = sink_ref[4 * g + hh] * LOG2E
            sh = scores[g][AB * hh:AB * (hh + 1)]
            if bias is not None:
                nb = bias.shape[1]
                sh = jnp.concatenate([sh[:, :nb] + bias, sh[:, nb:]], axis=1)
            m = jnp.maximum(jnp.max(sh, axis=1, keepdims=True), sk)
            e = jnp.exp2(sh - m).astype(BF16)
            pv = jnp.dot(e, vsels[g][a], preferred_element_type=F32)
            den = pv[:, den_lane[a]:den_lane[a] + 1] + jnp.exp2(sk - m)
            halves[4 * g + hh] = pv * (1.0 / den)
    pairs = [jnp.where(low, halves[2 * p], halves[2 * p + 1]) for p in range(N_Q_HEADS // 2)]
    o_ref[...] = jnp.concatenate(pairs, axis=1).astype(BF16)


def _attn_lat_kernel(q_ref, km_ref, k0_ref, kp_ref, kc_ref, vm_ref, v0_ref, vp_ref, vc_ref, sink_ref, hm_ref, o_ref):
    i = pl.program_id(0)
    nb = pl.num_programs(0)
    kcat = jnp.concatenate([km_ref[...], k0_ref[...], kp_ref[...], kc_ref[...]], axis=0)
    vcat = jnp.concatenate([vm_ref[...], v0_ref[...], vp_ref[...], vc_ref[...]], axis=0)
    a = lax.broadcasted_iota(jnp.int32, (AB, 3 * AB), 0)
    r = lax.broadcasted_iota(jnp.int32, (AB, 3 * AB), 1)
    lo = jnp.where(i == 0, AB, 0)
    hi = jnp.where(i == nb - 1, 2 * AB - 1, 3 * AB - 1)
    lower = jnp.maximum(a + (AB - WINDOW), lo)
    upper = jnp.minimum(a + (AB + WINDOW), hi)
    bias = jnp.where(r >= lower, jnp.where(r <= upper, 0.0, NEG), NEG)
    _attn_core(q_ref[...], kcat, vcat, bias, sink_ref, hm_ref, o_ref)


def _attn_ctx_kernel(q_ref, kc_ref, vc_ref, sink_ref, hm_ref, o_ref):
    _attn_core(q_ref[...], kc_ref[...], vc_ref[...], None, sink_ref, hm_ref, o_ref)


def _head_masks():
    hm = np.zeros((4, 3 * AB + CTX, 128), np.float32)
    hm[0, :, :HEAD_DIM] = 1.0
    hm[1, :, HEAD_DIM:] = 1.0
    hm[2, :, HEAD_DIM] = 1.0
    hm[3, :, 0] = 1.0
    return hm


def _attn_lat(q, k, v, sink, hm):
    nb = SEQ // AB
    off = CTX // AB
    last = ROWS // AB - 1
    kv = lambda f: pl.BlockSpec((AB, 2 * KV_W), f)
    prev = lambda i: (i + off - 1, 0)
    cur = lambda i: (i + off, 0)
    nxt = lambda i: (jnp.minimum(i + off + 1, last), 0)
    ctx = pl.BlockSpec((CTX, 2 * KV_W), lambda i: (0, 0))
    return pl.pallas_call(
        _attn_lat_kernel,
        grid=(nb,),
        in_specs=[pl.BlockSpec((AB, ATTN_W), cur), kv(prev), kv(cur), kv(nxt), ctx,
                  kv(prev), kv(cur), kv(nxt), ctx,
                  pl.BlockSpec(memory_space=pltpu.SMEM),
                  pl.BlockSpec((4, 3 * AB + CTX, 128), lambda i: (0, 0, 0))],
        out_specs=pl.BlockSpec((AB, ATTN_W), lambda i: (i, 0)),
        out_shape=jax.ShapeDtypeStruct((SEQ, ATTN_W), BF16),
        compiler_params=_params(("parallel",)),
        name="attn_window",
    )(q, k, k, k, k, v, v, v, v, sink, hm)


def _attn_ctx(q, k, v, sink, hm):
    ctx = pl.BlockSpec((CTX, 2 * KV_W), lambda i: (0, 0))
    return pl.pallas_call(
        _attn_ctx_kernel,
        grid=(CTX // AB,),
        in_specs=[pl.BlockSpec((AB, ATTN_W), lambda i: (i, 0)), ctx, ctx,
                  pl.BlockSpec(memory_space=pltpu.SMEM),
                  pl.BlockSpec((4, 3 * AB + CTX, 128), lambda i: (0, 0, 0))],
        out_specs=pl.BlockSpec((AB, ATTN_W), lambda i: (i, 0)),
        out_shape=jax.ShapeDtypeStruct((CTX, ATTN_W), BF16),
        compiler_params=_params(("parallel",)),
        name="attn_context",
    )(q, k, v, sink, hm)


def _outproj_kernel(*refs, with_ctx, with_router, split):
    x_in, refs = _load_rows(refs, split)
    it = iter(refs)
    mod_ref = next(it)
    al_ref = next(it)
    ac_ref = next(it) if with_ctx else None
    of_ref, ob_ref, hg_ref, gm_ref, w_ref, hgain_ref, bh_ref = (next(it) for _ in range(7))
    rt_ref = next(it) if with_router else None
    lt_ref = next(it) if with_router else None
    x1_ref = next(it)
    h2_ref = next(it)
    if with_router:
        gate_ref, rank_ref, cnt_ref, base_ref = (next(it) for _ in range(4))

    mod = mod_ref[0]
    attn = al_ref[...]
    if with_ctx:
        attn = jnp.where(pl.program_id(0) == 0, ac_ref[...], attn)
    o = of_ref[...] + ob_ref[...]
    ss = jnp.dot((o * o).astype(BF16), bh_ref[...], preferred_element_type=F32)
    hg = hg_ref[...]
    y = o * lax.rsqrt(ss * (1.0 / HG_K) + EPS) * hgain_ref[...] * (hg * jax.nn.sigmoid(hg))
    mix = jnp.concatenate([attn, y.astype(BF16), gm_ref[...]], axis=1)
    proj = jnp.dot(mix, w_ref[...], preferred_element_type=F32)
    x1 = x_in + mod[2:3] * proj
    x1_ref[...] = x1
    r = lax.rsqrt(jnp.mean(x1 * x1, axis=-1, keepdims=True) + EPS)
    h2 = (x1 * r) * (1.0 + mod[4:5]) + mod[3:4]
    h2_hi = h2.astype(BF16)
    h2_ref[...] = h2_hi

    if with_router:
        h2_lo = (h2 - h2_hi.astype(F32)).astype(BF16)
        two = jnp.dot(h2_hi, rt_ref[...], preferred_element_type=F32)
        logits = two[:, :128] + two[:, 128:] + jnp.dot(h2_lo, rt_ref[:, :128], preferred_element_type=F32)
        lane = lax.broadcasted_iota(jnp.int32, logits.shape, 1)
        l1 = jnp.where(lane < N_EXPERTS, logits, NEG)
        m1 = jnp.max(l1, axis=1, keepdims=True)
        i1 = jnp.min(jnp.where(l1 == m1, lane, 128), axis=1, keepdims=True)
        l2 = jnp.where(lane == i1, NEG, l1)
        m2 = jnp.max(l2, axis=1, keepdims=True)
        i2 = jnp.min(jnp.where(l2 == m2, lane, 128), axis=1, keepdims=True)
        e2 = jnp.exp(m2 - m1)
        w1 = 1.0 / (1.0 + e2)
        gate_ref[...] = jnp.where(lane == i1, w1, 0.0) + jnp.where(lane == i2, e2 * w1, 0.0)

        @pl.when(pl.program_id(0) == 0)
        def _():
            base_ref[...] = jnp.zeros_like(base_ref)

        mem = jnp.where(lane == i1, 1.0, jnp.where(lane == i2, 1.0, 0.0))
        local = jnp.dot(lt_ref[...], mem.astype(BF16), preferred_element_type=F32)
        base = base_ref[...]
        rank_ref[...] = jnp.where(mem > 0.0, base + local, -1.0)
        after = base + jnp.sum(mem, axis=0, keepdims=True)
        cnt_ref[0, 0:1, :] = base
        cnt_ref[0, 1:2, :] = after
        cnt_ref[0, 2:8, :] = jnp.zeros((6, 128), F32)
        base_ref[...] = after


def _outproj(xs, modsel, attn_lat, attn_ctx, o_f, o_b, ph, gm, w_out, hgain, bh, router, *, with_ctx):
    with_router = router is not None
    off = 0 if with_ctx else CTX // TM
    nt = ROWS // TM - off
    rows_out = ROWS if with_ctx else SEQ
    row = lambda w: pl.BlockSpec((TM, w), lambda i: (i + off, 0))
    const2 = lambda i: (0, 0)
    x_specs, x_args = _row_specs(xs, D, off)
    in_specs = x_specs + [
        pl.BlockSpec((1, 8, D), lambda i: (jnp.minimum(i + off, 1), 0, 0)),
        pl.BlockSpec((TM, ATTN_W), lambda i: (jnp.maximum(i + off - CTX // TM, 0), 0))]
    args = x_args + [modsel, attn_lat]
    if with_ctx:
        in_specs.append(pl.BlockSpec((TM, ATTN_W), const2))
        args.append(attn_ctx)
    in_specs += [row(HG_W), row(HG_W),
                 pl.BlockSpec((TM, HG_W), lambda i: (i + off, 4)),
                 row(GM_W),
                 pl.BlockSpec((D, D), const2),
                 pl.BlockSpec((1, HG_W), const2),
                 pl.BlockSpec((HG_W, HG_W), const2)]
    args += [o_f, o_b, ph, gm, w_out, hgain, bh]
    out_specs = [pl.BlockSpec((TM, D), lambda i: (i, 0)), pl.BlockSpec((TM, D), lambda i: (i, 0))]
    out_shape = [jax.ShapeDtypeStruct((rows_out, D), F32), jax.ShapeDtypeStruct((rows_out, D), BF16)]
    scratch = []
    if with_router:
        t = np.arange(TM)
        strict_lower = jnp.asarray((t[None, :] < t[:, None]).astype(np.float32), BF16)
        in_specs += [pl.BlockSpec((D, 256), const2), pl.BlockSpec((TM, TM), const2)]
        args += [router, strict_lower]
        out_specs += [pl.BlockSpec((TM, 128), lambda i: (i, 0)), pl.BlockSpec((TM, 128), lambda i: (i, 0)),
                      pl.BlockSpec((1, 8, 128), lambda i: (i, 0, 0))]
        out_shape += [jax.ShapeDtypeStruct((rows_out, 128), F32), jax.ShapeDtypeStruct((rows_out, 128), F32),
                      jax.ShapeDtypeStruct((nt, 8, 128), F32)]
        scratch = [pltpu.VMEM((1, 128), F32)]
    return pl.pallas_call(
        functools.partial(_outproj_kernel, with_ctx=with_ctx, with_router=with_router, split=isinstance(xs, tuple)),
        grid=(nt,),
        in_specs=in_specs,
        out_specs=out_specs,
        out_shape=out_shape,
        scratch_shapes=scratch,
        compiler_params=_params(("arbitrary",) if with_router else ("parallel",)),
        name="outproj",
    )(*args)


def _ffn_kernel(h_ref, x_ref, mod_ref, wg_ref, wu_ref, wd_ref, o_ref):
    h = h_ref[...]
    acc = None
    for f in range(FFN_DIM // FFN_FC):
        cols = slice(f * FFN_FC, (f + 1) * FFN_FC)
        a = jnp.dot(h, wg_ref[:, cols], preferred_element_type=F32)
        b = jnp.dot(h, wu_ref[:, cols], preferred_element_type=F32)
        t = (a * jax.nn.sigmoid(a) * b).astype(BF16)
        y = jnp.dot(t, wd_ref[cols, :], preferred_element_type=F32)
        acc = y if acc is None else acc + y
    row = pl.program_id(0) * FFN_TM + lax.broadcasted_iota(jnp.int32, (FFN_TM, D), 0)
    gate = jnp.where(row < CTX, mod_ref[0][5:6], mod_ref[1][5:6])
    o_ref[...] = x_ref[...] + gate * acc


def _ffn(h2, x1, modsel, wg, wu, wd):
    nt = ROWS // FFN_TM
    resident = dict(pipeline_mode=pl.Buffered(1))
    return pl.pallas_call(
        _ffn_kernel,
        grid=(nt,),
        in_specs=[pl.BlockSpec((FFN_TM, D), lambda i: (i, 0)),
                  pl.BlockSpec((FFN_TM, D), lambda i: (i, 0)),
                  pl.BlockSpec((2, 8, D), lambda i: (0, 0, 0)),
                  pl.BlockSpec((D, FFN_DIM), lambda i: (0, 0), **resident),
                  pl.BlockSpec((D, FFN_DIM), lambda i: (0, 0), **resident),
                  pl.BlockSpec((FFN_DIM, D), lambda i: (0, 0), **resident)],
        out_specs=pl.BlockSpec((FFN_TM, D), lambda i: (i, 0)),
        out_shape=jax.ShapeDtypeStruct((ROWS, D), F32),
        compiler_params=_params(("parallel",)),
        name="ffn_dense",
    )(h2, x1, modsel, wg, wu, wd)


def _moe_index(i, s, ie, ibase, irows, cnt):
    valid = irows[i] > 0
    chunk = jnp.where(valid, jnp.minimum(s, MOE_NG - 1), MOE_NG - 1)
    fidx = jnp.where(valid, jnp.maximum(s - MOE_NG, 0), MOE_NF - 1)
    return chunk, fidx


def _moe_expert_kernel(ie_ref, ibase_ref, irows_ref, cnt_ref,
                       h_ref, rk_ref, wg_ref, wu_ref, wd_ref, ys_ref, yg_ref):
    i = pl.program_id(0)
    s = pl.program_id(1)
    e = ie_ref[i]
    base = ibase_ref[i]
    rows = irows_ref[i]
    ntile = (rows + MOE_TILE - 1) // MOE_TILE

    @pl.when(s == 0)
    def _():
        ys_ref[...] = jnp.zeros_like(ys_ref)

    def gather_chunk(j):
        c = s * (MOE_GB // MOE_GC) + j
        cols = slice(j * MOE_GC, (j + 1) * MOE_GC)
        lo = jnp.maximum(cnt_ref[c * N_EXPERTS + e] - base, 0)
        hi = jnp.minimum(cnt_ref[(c + 1) * N_EXPERTS + e] - base, rows) - 1
        a_lo = lo // MOE_GT
        a_hi = jnp.where(hi >= lo, hi // MOE_GT, a_lo - 1)
        rel = rk_ref[pl.ds(e, 1), cols].astype(jnp.int32) - base
        h = h_ref[cols, :]

        def tile_body(a, carry):
            off = pl.multiple_of(a * MOE_GT, MOE_GT)
            slot = lax.broadcasted_iota(jnp.int32, (MOE_GT, MOE_GC), 0) + off
            onehot = jnp.where(rel == slot, 1.0, 0.0).astype(BF16)
            packed = jnp.dot(onehot, h, preferred_element_type=F32)
            ys_ref[pl.ds(off, MOE_GT), :] = (ys_ref[pl.ds(off, MOE_GT), :].astype(F32) + packed).astype(BF16)
            return carry

        lax.fori_loop(a_lo, a_hi + 1, tile_body, 0)

    @pl.when((s < MOE_NG) & (rows > 0))
    def _():
        for j in range(MOE_GB // MOE_GC):
            gather_chunk(j)

    def expert_step(mode):
        wg = wg_ref[0, 0].astype(BF16)
        wu = wu_ref[0, 0].astype(BF16)
        wd = wd_ref[0, 0].astype(BF16)

        def tile_body(k, carry):
            off = pl.multiple_of(k * MOE_TILE, MOE_TILE)
            xg = ys_ref[pl.ds(off, MOE_TILE), :]
            a = jnp.dot(xg, wg, preferred_element_type=F32)
            b = jnp.dot(xg, wu, preferred_element_type=F32)
            y = jnp.dot((a * jax.nn.sigmoid(a) * b).astype(BF16), wd, preferred_element_type=F32)
            if mode == "first":
                yg_ref[pl.ds(off, MOE_TILE), :] = y
            elif mode == "middle":
                yg_ref[pl.ds(off, MOE_TILE), :] += y
            else:
                ys_ref[pl.ds(off, MOE_TILE), :] = (yg_ref[pl.ds(off, MOE_TILE), :] + y).astype(BF16)
            return carry

        lax.fori_loop(0, ntile, tile_body, 0)

    f = s - MOE_NG
    pl.when((f == 0) & (rows > 0))(functools.partial(expert_step, "first"))
    pl.when((f > 0) & (f < MOE_NF - 1) & (rows > 0))(functools.partial(expert_step, "middle"))
    pl.when((f == MOE_NF - 1) & (rows > 0))(functools.partial(expert_step, "last"))


def _moe_experts(tables, h2, rk_t, wg, wu, wd):
    ie, ibase, irows, cnt = tables

    def hmap(i, s, ie, ibase, irows, cnt):
        return (_moe_index(i, s, ie, ibase, irows, cnt)[0], 0)

    def rmap(i, s, ie, ibase, irows, cnt):
        return (0, _moe_index(i, s, ie, ibase, irows, cnt)[0])

    def upmap(i, s, ie, ibase, irows, cnt):
        return (0, ie[i], 0, _moe_index(i, s, ie, ibase, irows, cnt)[1])

    def downmap(i, s, ie, ibase, irows, cnt):
        return (0, ie[i], _moe_index(i, s, ie, ibase, irows, cnt)[1], 0)

    grid_spec = pltpu.PrefetchScalarGridSpec(
        num_scalar_prefetch=4,
        grid=(MOE_ITEMS, MOE_NG + MOE_NF),
        in_specs=[pl.BlockSpec((MOE_GB, D), hmap),
                  pl.BlockSpec((N_EXPERTS, MOE_GB), rmap),
                  pl.BlockSpec((1, 1, D, MOE_FC), upmap),
                  pl.BlockSpec((1, 1, D, MOE_FC), upmap),
                  pl.BlockSpec((1, 1, MOE_FC, D), downmap)],
        out_specs=pl.BlockSpec((MOE_CAP, D), lambda i, s, ie, ibase, irows, cnt: (i, 0)),
        scratch_shapes=[pltpu.VMEM((MOE_CAP, D), F32)],
    )
    return pl.pallas_call(
        _moe_expert_kernel,
        grid_spec=grid_spec,
        out_shape=jax.ShapeDtypeStruct((MOE_ITEMS * MOE_CAP, D), BF16),
        compiler_params=_params(("arbitrary", "arbitrary"), vmem=MOE_VMEM_LIMIT),
        name="moe_experts",
    )(ie, ibase, irows, cnt, h2, rk_t, wg, wu, wd)


def _moe_combine_kernel(tix_ref, se_ref, f0_ref, f1_ref, f2_ref, wide_ref, x_ref, gate_ref, rk_ref, gf_ref, *refs):
    y_refs, o_ref = refs[:-1], refs[-1]
    w = pl.program_id(0)

    def unpack(ntile):
        gate = gate_ref[...]
        rk = rk_ref[...]
        lane = lax.broadcasted_iota(jnp.int32, gate.shape, 1)
        slot = lax.broadcasted_iota(jnp.int32, (MOE_CW, ntile * MOE_CT), 1)
        acc = jnp.zeros((MOE_CW, D), F32)
        for e in range(N_EXPERTS):
            rcol = jnp.sum(jnp.where(lane == e, rk, 0.0), axis=1, keepdims=True)
            gcol = jnp.sum(jnp.where(lane == e, gate, 0.0), axis=1, keepdims=True)
            pos = rcol.astype(jnp.int32) + (se_ref[e] - tix_ref[w * N_EXPERTS + e] * MOE_CT)
            pos = jnp.where(rcol < 0.0, -1, pos)
            onehot = jnp.where(pos == slot, 1.0, 0.0).astype(BF16)
            ycat = jnp.concatenate([y_refs[MOE_CS * e + j][...] for j in range(ntile)], axis=0)
            acc = acc + gcol * jnp.dot(onehot, ycat, preferred_element_type=F32)
        o_ref[...] = x_ref[...] + gf_ref[...] * acc

    pl.when(wide_ref[w] == 0)(functools.partial(unpack, MOE_CS - 1))
    pl.when(wide_ref[w] != 0)(functools.partial(unpack, MOE_CS))


def _moe_combine(ctab, x1, gate, rk, gf, ys):
    assert MOE_CS == 3
    row = lambda wd_: pl.BlockSpec((MOE_CW, wd_), lambda w, *tabs: (w, 0))

    def ymap(e, j):
        return lambda w, *tabs: (tabs[2 + j][w * N_EXPERTS + e], 0)

    y_specs = [pl.BlockSpec((MOE_CT, D), ymap(e, j)) for e in range(N_EXPERTS) for j in range(MOE_CS)]
    grid_spec = pltpu.PrefetchScalarGridSpec(
        num_scalar_prefetch=len(ctab),
        grid=(SEQ // MOE_CW,),
        in_specs=[row(D), row(128), row(128), pl.BlockSpec((1, D), lambda w, *tabs: (0, 0))] + y_specs,
        out_specs=row(D),
    )
    return pl.pallas_call(
        _moe_combine_kernel,
        grid_spec=grid_spec,
        out_shape=jax.ShapeDtypeStruct((SEQ, D), F32),
        compiler_params=_params(("arbitrary",)),
        name="moe_combine",
    )(*ctab, x1, gate, rk, gf, *([ys] * (N_EXPERTS * MOE_CS)))


def _moe_tables(cnt):
    before = cnt[:, 0, :N_EXPERTS].astype(jnp.int32)
    total = cnt[-1, 1, :N_EXPERTS].astype(jnp.int32)
    nit = (total + MOE_CAP - 1) // MOE_CAP
    ends = jnp.cumsum(nit)
    first = ends - nit
    item = jnp.arange(MOE_ITEMS, dtype=jnp.int32)
    n_items = ends[-1]
    ie = jnp.minimum(jnp.sum((ends[None, :] <= item[:, None]).astype(jnp.int32), axis=1), N_EXPERTS - 1)
    last_e = ie[jnp.maximum(n_items - 1, 0)]
    valid = item < n_items
    ibase = jnp.where(valid, (item - first[ie]) * MOE_CAP, 0)
    irows = jnp.where(valid, jnp.clip(total[ie] - ibase, 0, MOE_CAP), 0)
    ie = jnp.where(valid, ie, last_e)
    per_chunk = MOE_GC // TM
    cnt_g = jnp.concatenate([before[::per_chunk], total[None, :]], axis=0).reshape(-1)
    se = first * MOE_CAP
    per_win = MOE_CW // TM
    win_before = before[::per_win]
    win_rows = jnp.concatenate([win_before[1:], total[None, :]], axis=0) - win_before
    start = se[None, :] + win_before
    tix = jnp.minimum(start // MOE_CT, n_items * (MOE_CAP // MOE_CT) - MOE_CS)
    tix = jnp.maximum(tix, 0)
    end = start - tix * MOE_CT + win_rows
    nw = tix.shape[0]
    earlier = (jnp.arange(nw)[None, :] <= jnp.arange(nw)[:, None])[:, :, None]
    fetch = []
    for j in range(MOE_CS):
        want = jnp.where(end > j * MOE_CT, tix + j, 0)
        fetch.append(jnp.max(jnp.where(earlier, want[None, :, :], 0), axis=1).reshape(-1).astype(jnp.int32))
    etab = (ie, ibase.astype(jnp.int32), irows.astype(jnp.int32), cnt_g)
    wide = jnp.max((end > (MOE_CS - 1) * MOE_CT).astype(jnp.int32), axis=1)
    ctab = (tix.reshape(-1).astype(jnp.int32), se.astype(jnp.int32)) + tuple(fetch) + (wide,)
    return etab, ctab


def _rope_tables():
    rows = SEQ // GRID_W
    row = np.repeat(np.arange(rows, dtype=np.float32), GRID_W)
    col = np.tile(np.arange(GRID_W, dtype=np.float32), rows)
    axis_dim = HEAD_DIM // 2
    inv_freq = (np.float32(ROPE_THETA) ** (-np.arange(0, axis_dim, 2, dtype=np.float32) / np.float32(axis_dim)))
    ang_r = row[:, None] * inv_freq.astype(np.float32)
    ang_c = col[:, None] * inv_freq.astype(np.float32)
    cos_r, sin_r, cos_c, sin_c = np.cos(ang_r), np.sin(ang_r), np.cos(ang_c), np.sin(ang_c)
    cs = np.concatenate([cos_r, cos_r, cos_c, cos_c], axis=1)
    sn = np.concatenate([-sin_r, sin_r, -sin_c, sin_c], axis=1)
    cs = np.concatenate([np.ones((CTX, HEAD_DIM), np.float32), cs], axis=0)
    sn = np.concatenate([np.zeros((CTX, HEAD_DIM), np.float32), sn], axis=0)
    return (jnp.asarray(np.tile(cs, (1, 2)).astype(np.float32)), jnp.asarray(np.tile(sn, (1, 2)).astype(np.float32)))


def kernel(x, c, ctx, c_ctx, w_ada, b_ada, w_in, w_out, q_norm_gain, k_norm_gain, attn_sink, hgrn_lower_bound, hgrn_out_gain, gmlp_w_s, gmlp_b_s, gmlp_norm_gain, ffn_w_gate, ffn_w_up, ffn_w_down, moe_router, moe_w_gate, moe_w_up, moe_w_down):
    cc = jnp.concatenate([c, c_ctx[None, :], jnp.zeros((6, D), F32)], axis=0)
    mod = _ada(cc, w_ada, b_ada)
    mod6 = mod[:, :2].reshape(DEPTH, 2, 6, D)
    modsel = jnp.pad(jnp.stack([mod6[:, 1], mod6[:, 0]], axis=1), ((0, 0), (0, 0), (0, 2), (0, 0)))

    lb_soft = jax.nn.softmax(hgrn_lower_bound.astype(F32), axis=1)
    lower = jnp.cumsum(lb_soft, axis=1) - lb_soft[:, :1]

    cs, sn = _rope_tables()
    hconsts = _hgrn_consts()
    hconsts = tuple(jnp.asarray(a, BF16) for a in hconsts[:2]) + tuple(jnp.asarray(a, F32) for a in hconsts[2:4]) \
        + tuple(jnp.asarray(a, BF16) for a in hconsts[4:6]) + (jnp.asarray(hconsts[6], F32),
                                                               jnp.asarray(hconsts[7], BF16))
    bq = jnp.asarray(_block_ones(ATTN_W + KV_W, HEAD_DIM), BF16)
    bg = jnp.asarray(_block_ones(GM_W, GM_DIM), BF16)
    hm = jnp.asarray(_head_masks(), BF16)

    xs = (ctx[0], x[0])
    for layer in range(DEPTH):
        last = layer == DEPTH - 1
        gain = jnp.concatenate([jnp.tile(q_norm_gain[layer], N_Q_HEADS) * (HEAD_DIM ** -0.5 * LOG2E),
                                jnp.tile(k_norm_gain[layer], N_KV_HEADS)])[None, :]
        bias = jnp.repeat(gmlp_b_s[layer].T, GM_DIM, axis=1)
        q, k, v, ph, gm = _inproj(xs, modsel[layer], w_in[layer].astype(BF16), cs, sn, gain, bq, bg,
                                  gmlp_norm_gain[layer][None, :], gmlp_w_s[layer].astype(BF16), bias)
        o_f, o_b = _hgrn(ph, lower[0, layer][None, :], lower[1, layer][None, :], hconsts)
        sink = attn_sink[layer]
        attn_lat = _attn_lat(q, k, v, sink, hm)
        hgain = jnp.tile(hgrn_out_gain[layer], HG_HEADS)[None, :]
        w_o = w_out[layer].astype(BF16)
        if not last:
            attn_ctx = _attn_ctx(q, k, v, sink, hm)
            x1, h2 = _outproj(xs, modsel[layer], attn_lat, attn_ctx, o_f, o_b, ph, gm, w_o, hgain, bg, None,
                              with_ctx=True)
        else:
            r32 = jnp.pad(moe_router[layer // 2], ((0, 0), (0, 128 - N_EXPERTS)))
            r_hi = r32.astype(BF16)
            router = jnp.concatenate([r_hi, (r32 - r_hi.astype(F32)).astype(BF16)], axis=1)
            x1, h2, gate, rk, cnt = _outproj(xs, modsel[layer], attn_lat, None, o_f, o_b, ph, gm, w_o, hgain, bg,
                                             router, with_ctx=False)
        i = layer // 2
        if layer % 2 == 0:
            xs = _ffn(h2, x1, modsel[layer], ffn_w_gate[i].astype(BF16), ffn_w_up[i].astype(BF16),
                      ffn_w_down[i].astype(BF16))
        else:
            etab, ctab = _moe_tables(cnt)
            ys = _moe_experts(etab, h2, rk[:, :N_EXPERTS].T, moe_w_gate[i:i + 1], moe_w_up[i:i + 1],
                              moe_w_down[i:i + 1])
            xs = _moe_combine(ctab, x1, gate, rk, modsel[layer, 1, 5][None, :], ys)
    return xs[None]
```

```python
import functools

import numpy as np
import jax
import jax.numpy as jnp
from jax import lax
from jax.experimental import pallas as pl
from jax.experimental.pallas import tpu as pltpu

F32 = jnp.float32
BF16 = jnp.bfloat16

D = 1024
SEQ = 16384
CTX = 256
ROWS = CTX + SEQ
DEPTH = 2
GRID_W = 64
EPS = 1e-6
HEAD_DIM = 64
N_Q_HEADS = 8
N_KV_HEADS = 2
ATTN_W = 512
KV_W = 128
WINDOW = 128
ROPE_THETA = 10000.0
HG_HEADS = 4
HG_K = 64
HG_W = 256
GM_GROUPS = 4
GM_DIM = 64
GM_W = 256
GM_CHUNK = 128
IN_W = 2560
FFN_DIM = 2816
N_EXPERTS = 8
EXPERT_DIM = 3584

VMEM_LIMIT = 56 * 1024 * 1024
NEG = -1e30
LOG2E = 1.4426950408889634

TM = 256
HT = 128
HG_SUB = 2
HG_LEVELS = (8, 16, 32, 64)
AB = 128
FFN_TM = 640
FFN_FC = 1408
MOE_TILE = 256
MOE_CAP = 17 * MOE_TILE
MOE_ITEMS = 2 * SEQ // MOE_CAP + N_EXPERTS
MOE_GC = 256
MOE_GT = 128
MOE_GB = 2048
MOE_NG = SEQ // MOE_GB
MOE_FC = 512
MOE_NF = EXPERT_DIM // MOE_FC
MOE_CW = 256
MOE_CT = 128
MOE_CS = MOE_CW // MOE_CT + 1
MOE_VMEM_LIMIT = 62 * 1024 * 1024


def _params(sem, vmem=VMEM_LIMIT):
    return pltpu.CompilerParams(dimension_semantics=sem, vmem_limit_bytes=vmem)


def _block_ones(n, seg):
    i = np.arange(n)
    return (i[:, None] // seg == i[None, :] // seg).astype(np.float32)


def _ada_kernel(a_ref, w_ref, b_ref, o_ref):
    a = a_ref[...]
    s = (a * jax.nn.sigmoid(a)).astype(BF16)
    w = w_ref[0].astype(BF16)
    o_ref[0] = jnp.dot(s, w, preferred_element_type=F32) + b_ref[0]


def _ada(cc, w_ada, b_ada):
    tn = 1536
    return pl.pallas_call(
        _ada_kernel,
        grid=(DEPTH, 6 * D // tn),
        in_specs=[
            pl.BlockSpec((8, D), lambda l, n: (0, 0)),
            pl.BlockSpec((1, D, tn), lambda l, n: (l, 0, n)),
            pl.BlockSpec((1, 1, tn), lambda l, n: (l, 0, n)),
        ],
        out_specs=pl.BlockSpec((1, 8, tn), lambda l, n: (l, 0, n)),
        out_shape=jax.ShapeDtypeStruct((DEPTH, 8, 6 * D), F32),
        compiler_params=_params(("parallel", "parallel")),
        name="ada_mod",
    )(cc, w_ada, b_ada.reshape(DEPTH, 1, 6 * D))


def _gelu(x):
    return 0.5 * x * (1.0 + lax.erf(x * 0.7071067811865476))


def _load_rows(refs, split):
    if not split:
        return refs[0][...], refs[1:]
    return jnp.where(pl.program_id(0) == 0, refs[0][...], refs[1][...]), refs[2:]


def _row_specs(xs, width, off=0):
    if not isinstance(xs, tuple):
        return [pl.BlockSpec((TM, width), lambda i: (i + off, 0))], [xs]
    assert off == 0 and CTX == TM
    return [pl.BlockSpec((TM, width), lambda i: (0, 0)),
            pl.BlockSpec((TM, width), lambda i: (jnp.maximum(i - 1, 0), 0))], list(xs)


def _inproj_kernel(*refs, split):
    x, refs = _load_rows(refs, split)
    (mod_ref, w_ref, cs_ref, sn_ref, gain_ref, bq_ref, bg_ref, gng_ref, ws_ref, bs_ref,
     q_ref, k_ref, v_ref, ph_ref, gm_ref, wb_ref) = refs

    @pl.when(pl.program_id(0) == 0)
    def _():
        wb_ref[...] = w_ref[0].astype(BF16)

    r = lax.rsqrt(jnp.mean(x * x, axis=-1, keepdims=True) + EPS)
    mod = mod_ref[0]
    h = (x * r) * (1.0 + mod[1:2]) + mod[0:1]
    p = jnp.dot(h.astype(BF16), wb_ref[...], preferred_element_type=F32)

    qk = p[:, :ATTN_W + KV_W]
    ss = jnp.dot((qk * qk).astype(BF16), bq_ref[...], preferred_element_type=F32)
    qkn = qk * lax.rsqrt(ss * (1.0 / HEAD_DIM) + EPS) * gain_ref[...]
    lane = lax.broadcasted_iota(jnp.int32, (TM, 128), 1)
    first_half = (lane // 16) % 2 == 0
    cs = cs_ref[...]
    sn = sn_ref[...]
    rot = []
    for j in range(5):
        xs = qkn[:, 128 * j:128 * (j + 1)]
        partner = jnp.where(first_half, pltpu.roll(xs, 112, 1), pltpu.roll(xs, 16, 1))
        rot.append(xs * cs + partner * sn)
    q_ref[...] = jnp.concatenate(rot[:4], axis=1).astype(BF16)

    low = lane < 64

    def dup(t):
        sw = pltpu.roll(t, 64, 1)
        return jnp.concatenate([jnp.where(low, t, sw), jnp.where(low, sw, t)], axis=1)

    k_ref[...] = dup(rot[4]).astype(BF16)
    v_ref[...] = dup(p[:, ATTN_W + KV_W:ATTN_W + 2 * KV_W]).astype(BF16)

    ph_ref[...] = p[:, 768:2048]

    u = _gelu(p[:, 2048:2304])
    vv = _gelu(p[:, 2304:2560])
    ssv = jnp.dot((vv * vv).astype(BF16), bg_ref[...], preferred_element_type=F32)
    vn = vv * lax.rsqrt(ssv * (1.0 / GM_DIM) + EPS) * gng_ref[...]
    group = lax.broadcasted_iota(jnp.int32, (GM_CHUNK, GM_W), 1) // GM_DIM
    for c in range(TM // GM_CHUNK):
        vc = vn[GM_CHUNK * c:GM_CHUNK * (c + 1)]
        acc = bs_ref[...]
        for g in range(GM_GROUPS):
            vg = jnp.where(group == g, vc, 0.0).astype(BF16)
            acc = acc + jnp.dot(ws_ref[g], vg, preferred_element_type=F32)
        gm_ref[GM_CHUNK * c:GM_CHUNK * (c + 1), :] = (u[GM_CHUNK * c:GM_CHUNK * (c + 1)] * acc).astype(BF16)


def _inproj(xs, modsel, w_in, layer, cs, sn, gain, bq, bg, gng, ws, bs):
    nt = ROWS // TM
    const2 = lambda i: (0, 0)
    x_specs, x_args = _row_specs(xs, D)
    return pl.pallas_call(
        functools.partial(_inproj_kernel, split=isinstance(xs, tuple)),
        grid=(nt,),
        in_specs=x_specs + [
            pl.BlockSpec((1, 8, D), lambda i: (jnp.minimum(i, 1), 0, 0)),
            pl.BlockSpec((1, D, IN_W), lambda i: (layer, 0, 0), pipeline_mode=pl.Buffered(1)),
            pl.BlockSpec((TM, 128), lambda i: (i, 0)),
            pl.BlockSpec((TM, 128), lambda i: (i, 0)),
            pl.BlockSpec((1, ATTN_W + KV_W), const2),
            pl.BlockSpec((ATTN_W + KV_W, ATTN_W + KV_W), const2),
            pl.BlockSpec((GM_W, GM_W), const2),
            pl.BlockSpec((1, GM_W), const2),
            pl.BlockSpec((GM_GROUPS, GM_CHUNK, GM_CHUNK), lambda i: (0, 0, 0)),
            pl.BlockSpec((GM_CHUNK, GM_W), const2),
        ],
        out_specs=[
            pl.BlockSpec((TM, ATTN_W), lambda i: (i, 0)),
            pl.BlockSpec((TM, 2 * KV_W), lambda i: (i, 0)),
            pl.BlockSpec((TM, 2 * KV_W), lambda i: (i, 0)),
            pl.BlockSpec((TM, 5 * HG_W), lambda i: (i, 0)),
            pl.BlockSpec((TM, GM_W), lambda i: (i, 0)),
        ],
        out_shape=[
            jax.ShapeDtypeStruct((ROWS, ATTN_W), BF16),
            jax.ShapeDtypeStruct((ROWS, 2 * KV_W), BF16),
            jax.ShapeDtypeStruct((ROWS, 2 * KV_W), BF16),
            jax.ShapeDtypeStruct((ROWS, 5 * HG_W), F32),
            jax.ShapeDtypeStruct((ROWS, GM_W), BF16),
        ],
        scratch_shapes=[pltpu.VMEM((D, IN_W), BF16)],
        compiler_params=_params(("arbitrary",)),
        name="inproj",
    )(*x_args, modsel, w_in, cs, sn, gain, bq, bg, gng, ws, bs)


def _hgrn_consts():
    t = np.arange(HT)
    tt, ss = t[:, None], t[None, :]
    stack, masks = [], []
    for m in HG_LEVELS:
        same = (tt // m) == (ss // m)
        stack += [same & (ss <= tt), same & (ss > tt)]
        masks.append(((tt // (2 * m)) == (ss // (2 * m))) & ((tt % (2 * m)) >= m) & ((ss % (2 * m)) < m))
    stack.append(ss <= tt)
    masks.append(((tt // 8) == (ss // 8)) & (ss <= tt))
    lst_f = np.tile(np.concatenate(stack, axis=0).astype(np.float32), (1, 2))
    lst_b = np.tile(np.concatenate([b.T for b in stack], axis=0).astype(np.float32), (1, 2))
    msk_f = np.stack([np.tile(m, (1, HG_HEADS)) for m in masks]).astype(np.float32)
    msk_b = np.stack([np.tile(m.T, (1, HG_HEADS)) for m in masks]).astype(np.float32)
    e = np.zeros((8, HG_W, 128), np.float32)
    tl = np.zeros((128, HG_HEADS * HT), np.float32)
    for h in range(HG_HEADS):
        for j in range(8):
            e[j, h * HG_K:(h + 1) * HG_K, h * 8 + j] = 1.0
            tl[h * 8 + j, h * HT + t[(t % 8) == j]] = 1.0
    lane_head = np.arange(HG_W) // HG_K
    hm = np.stack([np.tile((lane_head == h).astype(np.float32), (HT, 1)) for h in range(HG_HEADS)])
    return lst_f, lst_b, msk_f, msk_b, e.reshape(8 * HG_W, 128), tl, _block_ones(HG_W, HG_K), hm


def _hgrn_direction(zq, zf, vi, lb, lst_ref, msk_ref, e_ref, tl_ref, bd_ref, hm_ref, st_ref, q3_ref, k3_ref, p3_ref):
    q = (zq * jax.nn.sigmoid(zq)) * (HG_K ** -0.5)
    f = lb + (1.0 - lb) * jax.nn.sigmoid(zf)
    g = jnp.log(f) * LOG2E
    kk = 1.0 - f
    g_hi = g.astype(BF16)
    g_lo = (g - g_hi.astype(F32)).astype(BF16)
    cum = jnp.dot(lst_ref[...], jnp.concatenate([g_hi, g_lo], axis=0), preferred_element_type=F32)
    nl = len(HG_LEVELS)
    c = cum[2 * nl * HT:(2 * nl + 1) * HT]
    ctot = jnp.sum(g, axis=0, keepdims=True)
    nt_dims = (((1,), (1,)), ((), ()))

    def per_head_rows(t):
        tb = t.astype(BF16)
        return jnp.concatenate([tb * hm_ref[h] for h in range(HG_HEADS)], axis=0)

    yield
    st = st_ref[...]
    qe = (q * jnp.exp2(c)).astype(BF16)
    o = lax.dot_general(qe, st.astype(BF16), nt_dims, preferred_element_type=F32)
    ke = (kk * jnp.exp2(ctot - c)).astype(BF16)
    upd = lax.dot_general(vi.astype(BF16), ke, (((0,), (0,)), ((), ())), preferred_element_type=F32)
    st_ref[...] = st * jnp.exp2(ctot) + upd * bd_ref[...]
    yield

    p8 = cum[0:HT]
    q3_ref[...] = q.reshape(HT // 8, 8, HG_W)
    k3_ref[...] = kk.reshape(HT // 8, 8, HG_W)
    p3_ref[...] = p8.reshape(HT // 8, 8, HG_W)
    slabs = []
    for j in range(8):
        kj = k3_ref[:, j:j + 1, :]
        pj = p3_ref[:, j:j + 1, :]
        slab = q3_ref[...] * kj * jnp.exp2(jnp.minimum(p3_ref[...] - pj, 0.0))
        slabs.append(slab.reshape(HT, HG_W).astype(BF16))
        if j % 2 == 1:
            yield
    a = jnp.dot(jnp.concatenate(slabs, axis=1), e_ref[...], preferred_element_type=F32)
    sc = msk_ref[nl] * jnp.dot(a.astype(BF16), tl_ref[...], preferred_element_type=F32)

    for li in range(nl):
        yield
        pm = cum[2 * li * HT:(2 * li + 1) * HT]
        rm = cum[(2 * li + 1) * HT:(2 * li + 2) * HT]
        qt = (q * jnp.exp2(pm)).astype(BF16)
        kt = per_head_rows(kk * jnp.exp2(rm))
        sc = sc + msk_ref[li] * lax.dot_general(qt, kt, nt_dims, preferred_element_type=F32)

    yield
    return o + jnp.dot(sc.astype(BF16), per_head_rows(vi), preferred_element_type=F32)


def _hgrn_kernel(qf_ref, ff_ref, if_ref, qb_ref, fb_ref, ib_ref, lbf_ref, lbb_ref,
                 lstf_ref, lstb_ref, mskf_ref, mskb_ref, e_ref, tl_ref, bd_ref, hm_ref,
                 of_ref, ob_ref, stf_ref, stb_ref, *scr):
    @pl.when(pl.program_id(0) == 0)
    def _():
        stf_ref[...] = jnp.zeros_like(stf_ref)
        stb_ref[...] = jnp.zeros_like(stb_ref)

    scans, dests = [], []
    for k in range(HG_SUB):
        rf = slice(k * HT, (k + 1) * HT)
        rb = slice((HG_SUB - 1 - k) * HT, (HG_SUB - k) * HT)
        s3 = scr[6 * k:6 * (k + 1)]
        scans.append(_hgrn_direction(qf_ref[rf, :], ff_ref[rf, :], if_ref[rf, :], lbf_ref[...], lstf_ref, mskf_ref,
                                     e_ref, tl_ref, bd_ref, hm_ref, stf_ref, *s3[:3]))
        dests.append((of_ref, rf))
        scans.append(_hgrn_direction(qb_ref[rb, :], fb_ref[rb, :], ib_ref[rb, :], lbb_ref[...], lstb_ref, mskb_ref,
                                     e_ref, tl_ref, bd_ref, hm_ref, stb_ref, *s3[3:]))
        dests.append((ob_ref, rb))
    live = list(range(len(scans)))
    while live:
        for d in list(live):
            try:
                next(scans[d])
            except StopIteration as done:
                ref, rows = dests[d]
                ref[rows, :] = done.value
                live.remove(d)


def _hgrn(ph, lbf, lbb, consts):
    lst_f, lst_b, msk_f, msk_b, e, tl, bd, hm = consts
    rows = HG_SUB * HT
    nt = ROWS // rows
    nc = CTX // rows

    def fwd(col):
        return pl.BlockSpec((rows, HG_W), lambda j: (j, col))

    def bwd(col):
        return pl.BlockSpec((rows, HG_W), lambda j: (jnp.where(j < nc, nc - 1 - j, nt + nc - 1 - j), col))

    def const(a):
        nd = a.ndim
        return pl.BlockSpec(a.shape, lambda j: (0,) * nd)

    scr3 = pltpu.VMEM((HT // 8, 8, HG_W), F32)
    return pl.pallas_call(
        _hgrn_kernel,
        grid=(nt,),
        in_specs=[fwd(0), fwd(1), fwd(3), bwd(0), bwd(2), bwd(3), const(lbf), const(lbb),
                  const(lst_f), const(lst_b), const(msk_f), const(msk_b), const(e), const(tl), const(bd), const(hm)],
        out_specs=[pl.BlockSpec((rows, HG_W), lambda j: (j, 0)),
                   pl.BlockSpec((rows, HG_W), lambda j: (jnp.where(j < nc, nc - 1 - j, nt + nc - 1 - j), 0))],
        out_shape=[jax.ShapeDtypeStruct((ROWS, HG_W), F32), jax.ShapeDtypeStruct((ROWS, HG_W), F32)],
        scratch_shapes=[pltpu.VMEM((HG_W, HG_W), F32), pltpu.VMEM((HG_W, HG_W), F32)] + [scr3] * (6 * HG_SUB),
        compiler_params=_params(("arbitrary",)),
        name="hgrn_scan",
    )(ph, ph, ph, ph, ph, ph, lbf, lbb, lst_f, lst_b, msk_f, msk_b, e, tl, bd, hm)


def _attn_core(q, kcat, vcat, bias, sink_ref, hm_ref, o_ref):
    nk = kcat.shape[0]
    nt_dims = (((1,), (1,)), ((), ()))
    low = lax.broadcasted_iota(jnp.int32, (AB, 128), 1) < HEAD_DIM
    den_lane = (HEAD_DIM, 0)
    scores, vsels = [], []
    for g in range(N_KV_HEADS):
        kg = kcat[:, 128 * g:128 * (g + 1)]
        vg = vcat[:, 128 * g:128 * (g + 1)]
        vsels.append((vg * hm_ref[0, :nk, :] + hm_ref[2, :nk, :], vg * hm_ref[1, :nk, :] + hm_ref[3, :nk, :]))
        qs = []
        for pr in (2 * g, 2 * g + 1):
            qp = q[:, 128 * pr:128 * (pr + 1)]
            qs += [qp * hm_ref[0, :AB, :], qp * hm_ref[1, :AB, :]]
        scores.append(lax.dot_general(jnp.concatenate(qs, axis=0), kg, nt_dims, preferred_element_type=F32))
    halves = {}
    for hh in range(N_Q_HEADS // N_KV_HEADS):
        for g in range(N_KV_HEADS):
            a = hh % 2
            sk = sink_ref[4 * g + hh] * LOG2E
            sh = scores[g][AB * hh:AB * (hh + 1)]
            if bias is not None:
                nb = bias.shape[1]
                sh = jnp.concatenate([sh[:, :nb] + bias, sh[:, nb:]], axis=1)
            m = jnp.maximum(jnp.max(sh, axis=1, keepdims=True), sk)
            e = jnp.exp2(sh - m).astype(BF16)
            pv = jnp.dot(e, vsels[g][a], preferred_element_type=F32)
            den = pv[:, den_lane[a]:den_lane[a] + 1] + jnp.exp2(sk - m)
            halves[4 * g + hh] = pv * (1.0 / den)
    pairs = [jnp.where(low, halves[2 * p], halves[2 * p + 1]) for p in range(N_Q_HEADS // 2)]
    o_ref[...] = jnp.concatenate(pairs, axis=1).astype(BF16)


def _attn_lat_kernel(q_ref, km_ref, k0_ref, kp_ref, kc_ref, vm_ref, v0_ref, vp_ref, vc_ref, sink_ref, hm_ref, o_ref):
    i = pl.program_id(0)
    nb = pl.num_programs(0)
    kcat = jnp.concatenate([km_ref[...], k0_ref[...], kp_ref[...], kc_ref[...]], axis=0)
    vcat = jnp.concatenate([vm_ref[...], v0_ref[...], vp_ref[...], vc_ref[...]], axis=0)
    a = lax.broadcasted_iota(jnp.int32, (AB, 3 * AB), 0)
    r = lax.broadcasted_iota(jnp.int32, (AB, 3 * AB), 1)
    lo = jnp.where(i == 0, AB, 0)
    hi = jnp.where(i == nb - 1, 2 * AB - 1, 3 * AB - 1)
    lower = jnp.maximum(a + (AB - WINDOW), lo)
    upper = jnp.minimum(a + (AB + WINDOW), hi)
    bias = jnp.where(r >= lower, jnp.where(r <= upper, 0.0, NEG), NEG)
    _attn_core(q_ref[...], kcat, vcat, bias, sink_ref, hm_ref, o_ref)


def _attn_ctx_kernel(q_ref, kc_ref, vc_ref, sink_ref, hm_ref, o_ref):
    _attn_core(q_ref[...], kc_ref[...], vc_ref[...], None, sink_ref, hm_ref, o_ref)


def _head_masks():
    hm = np.zeros((4, 3 * AB + CTX, 128), np.float32)
    hm[0, :, :HEAD_DIM] = 1.0
    hm[1, :, HEAD_DIM:] = 1.0
    hm[2, :, HEAD_DIM] = 1.0
    hm[3, :, 0] = 1.0
    return hm


def _attn_lat(q, k, v, sink, hm):
    nb = SEQ // AB
    off = CTX // AB
    last = ROWS // AB - 1
    kv = lambda f: pl.BlockSpec((AB, 2 * KV_W), f)
    prev = lambda i: (i + off - 1, 0)
    cur = lambda i: (i + off, 0)
    nxt = lambda i: (jnp.minimum(i + off + 1, last), 0)
    ctx = pl.BlockSpec((CTX, 2 * KV_W), lambda i: (0, 0))
    return pl.pallas_call(
        _attn_lat_kernel,
        grid=(nb,),
        in_specs=[pl.BlockSpec((AB, ATTN_W), cur), kv(prev), kv(cur), kv(nxt), ctx,
                  kv(prev), kv(cur), kv(nxt), ctx,
                  pl.BlockSpec(memory_space=pltpu.SMEM),
                  pl.BlockSpec((4, 3 * AB + CTX, 128), lambda i: (0, 0, 0))],
        out_specs=pl.BlockSpec((AB, ATTN_W), lambda i: (i, 0)),
        out_shape=jax.ShapeDtypeStruct((SEQ, ATTN_W), BF16),
        compiler_params=_params(("parallel",)),
        name="attn_window",
    )(q, k, k, k, k, v, v, v, v, sink, hm)


def _attn_ctx(q, k, v, sink, hm):
    ctx = pl.BlockSpec((CTX, 2 * KV_W), lambda i: (0, 0))
    return pl.pallas_call(
        _attn_ctx_kernel,
        grid=(CTX // AB,),
        in_specs=[pl.BlockSpec((AB, ATTN_W), lambda i: (i, 0)), ctx, ctx,
                  pl.BlockSpec(memory_space=pltpu.SMEM),
                  pl.BlockSpec((4, 3 * AB + CTX, 128), lambda i: (0, 0, 0))],
        out_specs=pl.BlockSpec((AB, ATTN_W), lambda i: (i, 0)),
        out_shape=jax.ShapeDtypeStruct((CTX, ATTN_W), BF16),
        compiler_params=_params(("parallel",)),
        name="attn_context",
    )(q, k, v, sink, hm)


def _outproj_kernel(*refs, with_ctx, with_router, split):
    x_in, refs = _load_rows(refs, split)
    it = iter(refs)
    mod_ref = next(it)
    al_ref = next(it)
    ac_ref = next(it) if with_ctx else None
    of_ref, ob_ref, hg_ref, gm_ref, w_ref, hgain_ref, bh_ref = (next(it) for _ in range(7))
    rt_ref = next(it) if with_router else None
    lt_ref = next(it) if with_router else None
    x1_ref = next(it)
    h2_ref = next(it)
    if with_router:
        gate_ref, rank_ref, cnt_ref, base_ref = (next(it) for _ in range(4))
    wb_ref = next(it)

    @pl.when(pl.program_id(0) == 0)
    def _():
        wb_ref[...] = w_ref[0].astype(BF16)

    mod = mod_ref[0]
    attn = al_ref[...]
    if with_ctx:
        attn = jnp.where(pl.program_id(0) == 0, ac_ref[...], attn)
    o = of_ref[...] + ob_ref[...]
    ss = jnp.dot((o * o).astype(BF16), bh_ref[...], preferred_element_type=F32)
    hg = hg_ref[...]
    y = o * lax.rsqrt(ss * (1.0 / HG_K) + EPS) * hgain_ref[...] * (hg * jax.nn.sigmoid(hg))
    mix = jnp.concatenate([attn, y.astype(BF16), gm_ref[...]], axis=1)
    proj = jnp.dot(mix, wb_ref[...], preferred_element_type=F32)
    x1 = x_in + mod[2:3] * proj
    x1_ref[...] = x1
    r = lax.rsqrt(jnp.mean(x1 * x1, axis=-1, keepdims=True) + EPS)
    h2 = (x1 * r) * (1.0 + mod[4:5]) + mod[3:4]
    h2_hi = h2.astype(BF16)
    h2_ref[...] = h2_hi

    if with_router:
        h2_lo = (h2 - h2_hi.astype(F32)).astype(BF16)
        two = jnp.dot(h2_hi, rt_ref[...], preferred_element_type=F32)
        logits = two[:, :128] + two[:, 128:] + jnp.dot(h2_lo, rt_ref[:, :128], preferred_element_type=F32)
        lane = lax.broadcasted_iota(jnp.int32, logits.shape, 1)
        l1 = jnp.where(lane < N_EXPERTS, logits, NEG)
        m1 = jnp.max(l1, axis=1, keepdims=True)
        i1 = jnp.min(jnp.where(l1 == m1, lane, 128), axis=1, keepdims=True)
        l2 = jnp.where(lane == i1, NEG, l1)
        m2 = jnp.max(l2, axis=1, keepdims=True)
        i2 = jnp.min(jnp.where(l2 == m2, lane, 128), axis=1, keepdims=True)
        e2 = jnp.exp(m2 - m1)
        w1 = 1.0 / (1.0 + e2)
        gate_ref[...] = jnp.where(lane == i1, w1, 0.0) + jnp.where(lane == i2, e2 * w1, 0.0)

        @pl.when(pl.program_id(0) == 0)
        def _():
            base_ref[...] = jnp.zeros_like(base_ref)

        mem = jnp.where(lane == i1, 1.0, jnp.where(lane == i2, 1.0, 0.0))
        local = jnp.dot(lt_ref[...], mem.astype(BF16), preferred_element_type=F32)
        base = base_ref[...]
        rank_ref[...] = jnp.where(mem > 0.0, base + local, -1.0)
        after = base + jnp.sum(mem, axis=0, keepdims=True)
        cnt_ref[0, 0:1, :] = base
        cnt_ref[0, 1:2, :] = after
        cnt_ref[0, 2:8, :] = jnp.zeros((6, 128), F32)
        base_ref[...] = after


def _outproj(xs, modsel, attn_lat, attn_ctx, o_f, o_b, ph, gm, w_out, layer, hgain, bh, router, *, with_ctx):
    with_router = router is not None
    off = 0 if with_ctx else CTX // TM
    nt = ROWS // TM - off
    rows_out = ROWS if with_ctx else SEQ
    row = lambda w: pl.BlockSpec((TM, w), lambda i: (i + off, 0))
    const2 = lambda i: (0, 0)
    x_specs, x_args = _row_specs(xs, D, off)
    in_specs = x_specs + [
        pl.BlockSpec((1, 8, D), lambda i: (jnp.minimum(i + off, 1), 0, 0)),
        pl.BlockSpec((TM, ATTN_W), lambda i: (jnp.maximum(i + off - CTX // TM, 0), 0))]
    args = x_args + [modsel, attn_lat]
    if with_ctx:
        in_specs.append(pl.BlockSpec((TM, ATTN_W), const2))
        args.append(attn_ctx)
    in_specs += [row(HG_W), row(HG_W),
                 pl.BlockSpec((TM, HG_W), lambda i: (i + off, 4)),
                 row(GM_W),
                 pl.BlockSpec((1, D, D), lambda i: (layer, 0, 0), pipeline_mode=pl.Buffered(1)),
                 pl.BlockSpec((1, HG_W), const2),
                 pl.BlockSpec((HG_W, HG_W), const2)]
    args += [o_f, o_b, ph, gm, w_out, hgain, bh]
    out_specs = [pl.BlockSpec((TM, D), lambda i: (i, 0)), pl.BlockSpec((TM, D), lambda i: (i, 0))]
    out_shape = [jax.ShapeDtypeStruct((rows_out, D), F32), jax.ShapeDtypeStruct((rows_out, D), BF16)]
    scratch = []
    if with_router:
        t = np.arange(TM)
        strict_lower = jnp.asarray((t[None, :] < t[:, None]).astype(np.float32), BF16)
        in_specs += [pl.BlockSpec((D, 256), const2), pl.BlockSpec((TM, TM), const2)]
        args += [router, strict_lower]
        out_specs += [pl.BlockSpec((TM, 128), lambda i: (i, 0)), pl.BlockSpec((TM, 128), lambda i: (i, 0)),
                      pl.BlockSpec((1, 8, 128), lambda i: (i, 0, 0))]
        out_shape += [jax.ShapeDtypeStruct((rows_out, 128), F32), jax.ShapeDtypeStruct((rows_out, 128), F32),
                      jax.ShapeDtypeStruct((nt, 8, 128), F32)]
        scratch = [pltpu.VMEM((1, 128), F32)]
    scratch.append(pltpu.VMEM((D, D), BF16))
    return pl.pallas_call(
        functools.partial(_outproj_kernel, with_ctx=with_ctx, with_router=with_router, split=isinstance(xs, tuple)),
        grid=(nt,),
        in_specs=in_specs,
        out_specs=out_specs,
        out_shape=out_shape,
        scratch_shapes=scratch,
        compiler_params=_params(("arbitrary",)),
        name="outproj",
    )(*args)


def _ffn_kernel(h_ref, x_ref, mod_ref, wg_ref, wu_ref, wd_ref, o_ref):
    h = h_ref[...]
    acc = None
    for f in range(FFN_DIM // FFN_FC):
        cols = slice(f * FFN_FC, (f + 1) * FFN_FC)
        a = jnp.dot(h, wg_ref[:, cols], preferred_element_type=F32)
        b = jnp.dot(h, wu_ref[:, cols], preferred_element_type=F32)
        t = (a * jax.nn.sigmoid(a) * b).astype(BF16)
        y = jnp.dot(t, wd_ref[cols, :], preferred_element_type=F32)
        acc = y if acc is None else acc + y
    row = pl.program_id(0) * FFN_TM + lax.broadcasted_iota(jnp.int32, (FFN_TM, D), 0)
    gate = jnp.where(row < CTX, mod_ref[0][5:6], mod_ref[1][5:6])
    o_ref[...] = x_ref[...] + gate * acc


def _ffn(h2, x1, modsel, wg, wu, wd):
    nt = ROWS // FFN_TM
    resident = dict(pipeline_mode=pl.Buffered(1))
    return pl.pallas_call(
        _ffn_kernel,
        grid=(nt,),
        in_specs=[pl.BlockSpec((FFN_TM, D), lambda i: (i, 0)),
                  pl.BlockSpec((FFN_TM, D), lambda i: (i, 0)),
                  pl.BlockSpec((2, 8, D), lambda i: (0, 0, 0)),
                  pl.BlockSpec((D, FFN_DIM), lambda i: (0, 0), **resident),
                  pl.BlockSpec((D, FFN_DIM), lambda i: (0, 0), **resident),
                  pl.BlockSpec((FFN_DIM, D), lambda i: (0, 0), **resident)],
        out_specs=pl.BlockSpec((FFN_TM, D), lambda i: (i, 0)),
        out_shape=jax.ShapeDtypeStruct((ROWS, D), F32),
        compiler_params=_params(("parallel",)),
        name="ffn_dense",
    )(h2, x1, modsel, wg, wu, wd)


def _moe_index(i, s, ie, ibase, irows, cnt):
    valid = irows[i] > 0
    chunk = jnp.where(valid, jnp.minimum(s, MOE_NG - 1), MOE_NG - 1)
    fidx = jnp.where(valid, jnp.maximum(s - MOE_NG, 0), MOE_NF - 1)
    return chunk, fidx


def _moe_expert_kernel(ie_ref, ibase_ref, irows_ref, cnt_ref,
                       h_ref, rk_ref, wg_ref, wu_ref, wd_ref, ys_ref, yg_ref):
    i = pl.program_id(0)
    s = pl.program_id(1)
    e = ie_ref[i]
    base = ibase_ref[i]
    rows = irows_ref[i]
    ntile = (rows + MOE_TILE - 1) // MOE_TILE

    @pl.when(s == 0)
    def _():
        ys_ref[...] = jnp.zeros_like(ys_ref)

    def gather_chunk(j):
        c = s * (MOE_GB // MOE_GC) + j
        cols = slice(j * MOE_GC, (j + 1) * MOE_GC)
        lo = jnp.maximum(cnt_ref[c * N_EXPERTS + e] - base, 0)
        hi = jnp.minimum(cnt_ref[(c + 1) * N_EXPERTS + e] - base, rows) - 1
        a_lo = lo // MOE_GT
        a_hi = jnp.where(hi >= lo, hi // MOE_GT, a_lo - 1)
        rel = rk_ref[pl.ds(e, 1), cols].astype(jnp.int32) - base
        h = h_ref[cols, :]

        def tile_body(a, carry):
            off = pl.multiple_of(a * MOE_GT, MOE_GT)
            slot = lax.broadcasted_iota(jnp.int32, (MOE_GT, MOE_GC), 0) + off
            onehot = jnp.where(rel == slot, 1.0, 0.0).astype(BF16)
            packed = jnp.dot(onehot, h, preferred_element_type=F32)
            ys_ref[pl.ds(off, MOE_GT), :] = (ys_ref[pl.ds(off, MOE_GT), :].astype(F32) + packed).astype(BF16)
            return carry

        lax.fori_loop(a_lo, a_hi + 1, tile_body, 0)

    @pl.when((s < MOE_NG) & (rows > 0))
    def _():
        for j in range(MOE_GB // MOE_GC):
            gather_chunk(j)

    def expert_step(mode):
        wg = wg_ref[0, 0].astype(BF16)
        wu = wu_ref[0, 0].astype(BF16)
        wd = wd_ref[0, 0].astype(BF16)

        def tile_body(k, carry):
            off = pl.multiple_of(k * MOE_TILE, MOE_TILE)
            xg = ys_ref[pl.ds(off, MOE_TILE), :]
            a = jnp.dot(xg, wg, preferred_element_type=F32)
            b = jnp.dot(xg, wu, preferred_element_type=F32)
            y = jnp.dot((a * jax.nn.sigmoid(a) * b).astype(BF16), wd, preferred_element_type=F32)
            if mode == "first":
                yg_ref[pl.ds(off, MOE_TILE), :] = y
            elif mode == "middle":
                yg_ref[pl.ds(off, MOE_TILE), :] += y
            else:
                ys_ref[pl.ds(off, MOE_TILE), :] = (yg_ref[pl.ds(off, MOE_TILE), :] + y).astype(BF16)
            return carry

        lax.fori_loop(0, ntile, tile_body, 0)

    f = s - MOE_NG
    pl.when((f == 0) & (rows > 0))(functools.partial(expert_step, "first"))
    pl.when((f > 0) & (f < MOE_NF - 1) & (rows > 0))(functools.partial(expert_step, "middle"))
    pl.when((f == MOE_NF - 1) & (rows > 0))(functools.partial(expert_step, "last"))


def _moe_experts(tables, h2, rk_t, wg, wu, wd):
    ie, ibase, irows, cnt = tables

    def hmap(i, s, ie, ibase, irows, cnt):
        return (_moe_index(i, s, ie, ibase, irows, cnt)[0], 0)

    def rmap(i, s, ie, ibase, irows, cnt):
        return (0, _moe_index(i, s, ie, ibase, irows, cnt)[0])

    def upmap(i, s, ie, ibase, irows, cnt):
        return (0, ie[i], 0, _moe_index(i, s, ie, ibase, irows, cnt)[1])

    def downmap(i, s, ie, ibase, irows, cnt):
        return (0, ie[i], _moe_index(i, s, ie, ibase, irows, cnt)[1], 0)

    grid_spec = pltpu.PrefetchScalarGridSpec(
        num_scalar_prefetch=4,
        grid=(MOE_ITEMS, MOE_NG + MOE_NF),
        in_specs=[pl.BlockSpec((MOE_GB, D), hmap),
                  pl.BlockSpec((N_EXPERTS, MOE_GB), rmap),
                  pl.BlockSpec((1, 1, D, MOE_FC), upmap),
                  pl.BlockSpec((1, 1, D, MOE_FC), upmap),
                  pl.BlockSpec((1, 1, MOE_FC, D), downmap)],
        out_specs=pl.BlockSpec((MOE_CAP, D), lambda i, s, ie, ibase, irows, cnt: (i, 0)),
        scratch_shapes=[pltpu.VMEM((MOE_CAP, D), F32)],
    )
    return pl.pallas_call(
        _moe_expert_kernel,
        grid_spec=grid_spec,
        out_shape=jax.ShapeDtypeStruct((MOE_ITEMS * MOE_CAP, D), BF16),
        compiler_params=_params(("arbitrary", "arbitrary"), vmem=MOE_VMEM_LIMIT),
        name="moe_experts",
    )(ie, ibase, irows, cnt, h2, rk_t, wg, wu, wd)


def _moe_combine_kernel(tix_ref, se_ref, f0_ref, f1_ref, f2_ref, wide_ref, x_ref, gate_ref, rk_ref, gf_ref, *refs):
    y_refs, o_ref = refs[:-1], refs[-1]
    w = pl.program_id(0)

    def unpack(ntile):
        gate = gate_ref[...]
        rk = rk_ref[...]
        lane = lax.broadcasted_iota(jnp.int32, gate.shape, 1)
        slot = lax.broadcasted_iota(jnp.int32, (MOE_CW, ntile * MOE_CT), 1)
        acc = jnp.zeros((MOE_CW, D), F32)
        for e in range(N_EXPERTS):
            rcol = jnp.sum(jnp.where(lane == e, rk, 0.0), axis=1, keepdims=True)
            gcol = jnp.sum(jnp.where(lane == e, gate, 0.0), axis=1, keepdims=True)
            pos = rcol.astype(jnp.int32) + (se_ref[e] - tix_ref[w * N_EXPERTS + e] * MOE_CT)
            pos = jnp.where(rcol < 0.0, -1, pos)
            onehot = jnp.where(pos == slot, 1.0, 0.0).astype(BF16)
            ycat = jnp.concatenate([y_refs[MOE_CS * e + j][...] for j in range(ntile)], axis=0)
            acc = acc + gcol * jnp.dot(onehot, ycat, preferred_element_type=F32)
        o_ref[...] = x_ref[...] + gf_ref[...] * acc

    pl.when(wide_ref[w] == 0)(functools.partial(unpack, MOE_CS - 1))
    pl.when(wide_ref[w] != 0)(functools.partial(unpack, MOE_CS))


def _moe_combine(ctab, x1, gate, rk, gf, ys):
    assert MOE_CS == 3
    row = lambda wd_: pl.BlockSpec((MOE_CW, wd_), lambda w, *tabs: (w, 0))

    def ymap(e, j):
        return lambda w, *tabs: (tabs[2 + j][w * N_EXPERTS + e], 0)

    y_specs = [pl.BlockSpec((MOE_CT, D), ymap(e, j)) for e in range(N_EXPERTS) for j in range(MOE_CS)]
    grid_spec = pltpu.PrefetchScalarGridSpec(
        num_scalar_prefetch=len(ctab),
        grid=(SEQ // MOE_CW,),
        in_specs=[row(D), row(128), row(128), pl.BlockSpec((1, D), lambda w, *tabs: (0, 0))] + y_specs,
        out_specs=row(D),
    )
    return pl.pallas_call(
        _moe_combine_kernel,
        grid_spec=grid_spec,
        out_shape=jax.ShapeDtypeStruct((SEQ, D), F32),
        compiler_params=_params(("arbitrary",)),
        name="moe_combine",
    )(*ctab, x1, gate, rk, gf, *([ys] * (N_EXPERTS * MOE_CS)))


def _moe_tables(cnt):
    before = cnt[:, 0, :N_EXPERTS].astype(jnp.int32)
    total = cnt[-1, 1, :N_EXPERTS].astype(jnp.int32)
    nit = (total + MOE_CAP - 1) // MOE_CAP
    ends = jnp.cumsum(nit)
    first = ends - nit
    item = jnp.arange(MOE_ITEMS, dtype=jnp.int32)
    n_items = ends[-1]
    ie = jnp.minimum(jnp.sum((ends[None, :] <= item[:, None]).astype(jnp.int32), axis=1), N_EXPERTS - 1)
    last_e = ie[jnp.maximum(n_items - 1, 0)]
    valid = item < n_items
    ibase = jnp.where(valid, (item - first[ie]) * MOE_CAP, 0)
    irows = jnp.where(valid, jnp.clip(total[ie] - ibase, 0, MOE_CAP), 0)
    ie = jnp.where(valid, ie, last_e)
    per_chunk = MOE_GC // TM
    cnt_g = jnp.concatenate([before[::per_chunk], total[None, :]], axis=0).reshape(-1)
    se = first * MOE_CAP
    per_win = MOE_CW // TM
    win_before = before[::per_win]
    win_rows = jnp.concatenate([win_before[1:], total[None, :]], axis=0) - win_before
    start = se[None, :] + win_before
    tix = jnp.minimum(start // MOE_CT, n_items * (MOE_CAP // MOE_CT) - MOE_CS)
    tix = jnp.maximum(tix, 0)
    end = start - tix * MOE_CT + win_rows
    nw = tix.shape[0]
    earlier = (jnp.arange(nw)[None, :] <= jnp.arange(nw)[:, None])[:, :, None]
    fetch = []
    for j in range(MOE_CS):
        want = jnp.where(end > j * MOE_CT, tix + j, 0)
        fetch.append(jnp.max(jnp.where(earlier, want[None, :, :], 0), axis=1).reshape(-1).astype(jnp.int32))
    etab = (ie, ibase.astype(jnp.int32), irows.astype(jnp.int32), cnt_g)
    wide = jnp.max((end > (MOE_CS - 1) * MOE_CT).astype(jnp.int32), axis=1)
    ctab = (tix.reshape(-1).astype(jnp.int32), se.astype(jnp.int32)) + tuple(fetch) + (wide,)
    return etab, ctab


def _rope_tables():
    rows = SEQ // GRID_W
    row = np.repeat(np.arange(rows, dtype=np.float32), GRID_W)
    col = np.tile(np.arange(GRID_W, dtype=np.float32), rows)
    axis_dim = HEAD_DIM // 2
    inv_freq = (np.float32(ROPE_THETA) ** (-np.arange(0, axis_dim, 2, dtype=np.float32) / np.float32(axis_dim)))
    ang_r = row[:, None] * inv_freq.astype(np.float32)
    ang_c = col[:, None] * inv_freq.astype(np.float32)
    cos_r, sin_r, cos_c, sin_c = np.cos(ang_r), np.sin(ang_r), np.cos(ang_c), np.sin(ang_c)
    cs = np.concatenate([cos_r, cos_r, cos_c, cos_c], axis=1)
    sn = np.concatenate([-sin_r, sin_r, -sin_c, sin_c], axis=1)
    cs = np.concatenate([np.ones((CTX, HEAD_DIM), np.float32), cs], axis=0)
    sn = np.concatenate([np.zeros((CTX, HEAD_DIM), np.float32), sn], axis=0)
    return (jnp.asarray(np.tile(cs, (1, 2)).astype(np.float32)), jnp.asarray(np.tile(sn, (1, 2)).astype(np.float32)))


def kernel(x, c, ctx, c_ctx, w_ada, b_ada, w_in, w_out, q_norm_gain, k_norm_gain, attn_sink, hgrn_lower_bound, hgrn_out_gain, gmlp_w_s, gmlp_b_s, gmlp_norm_gain, ffn_w_gate, ffn_w_up, ffn_w_down, moe_router, moe_w_gate, moe_w_up, moe_w_down):
    cc = jnp.concatenate([c, c_ctx[None, :], jnp.zeros((6, D), F32)], axis=0)
    mod = _ada(cc, w_ada, b_ada)
    mod6 = mod[:, :2].reshape(DEPTH, 2, 6, D)
    modsel = jnp.pad(jnp.stack([mod6[:, 1], mod6[:, 0]], axis=1), ((0, 0), (0, 0), (0, 2), (0, 0)))

    lb_soft = jax.nn.softmax(hgrn_lower_bound.astype(F32), axis=1)
    lower = jnp.cumsum(lb_soft, axis=1) - lb_soft[:, :1]

    cs, sn = _rope_tables()
    hconsts = _hgrn_consts()
    hconsts = tuple(jnp.asarray(a, BF16) for a in hconsts[:2]) + tuple(jnp.asarray(a, F32) for a in hconsts[2:4]) \
        + tuple(jnp.asarray(a, BF16) for a in hconsts[4:6]) + (jnp.asarray(hconsts[6], F32),
                                                               jnp.asarray(hconsts[7], BF16))
    bq = jnp.asarray(_block_ones(ATTN_W + KV_W, HEAD_DIM), BF16)
    bg = jnp.asarray(_block_ones(GM_W, GM_DIM), BF16)
    hm = jnp.asarray(_head_masks(), BF16)

    xs = (ctx[0], x[0])
    for layer in range(DEPTH):
        last = layer == DEPTH - 1
        gain = jnp.concatenate([jnp.tile(q_norm_gain[layer], N_Q_HEADS) * (HEAD_DIM ** -0.5 * LOG2E),
                                jnp.tile(k_norm_gain[layer], N_KV_HEADS)])[None, :]
        bias = jnp.repeat(gmlp_b_s[layer].T, GM_DIM, axis=1)
        q, k, v, ph, gm = _inproj(xs, modsel[layer], w_in, layer, cs, sn, gain, bq, bg,
                                  gmlp_norm_gain[layer][None, :], gmlp_w_s[layer].astype(BF16), bias)
        o_f, o_b = _hgrn(ph, lower[0, layer][None, :], lower[1, layer][None, :], hconsts)
        sink = attn_sink[layer]
        attn_lat = _attn_lat(q, k, v, sink, hm)
        hgain = jnp.tile(hgrn_out_gain[layer], HG_HEADS)[None, :]
        if not last:
            attn_ctx = _attn_ctx(q, k, v, sink, hm)
            x1, h2 = _outproj(xs, modsel[layer], attn_lat, attn_ctx, o_f, o_b, ph, gm, w_out, layer, hgain, bg, None,
                              with_ctx=True)
        else:
            r32 = jnp.pad(moe_router[layer // 2], ((0, 0), (0, 128 - N_EXPERTS)))
            r_hi = r32.astype(BF16)
            router = jnp.concatenate([r_hi, (r32 - r_hi.astype(F32)).astype(BF16)], axis=1)
            x1, h2, gate, rk, cnt = _outproj(xs, modsel[layer], attn_lat, None, o_f, o_b, ph, gm, w_out, layer, hgain,
                                             bg, router, with_ctx=False)
        i = layer // 2
        if layer % 2 == 0:
            xs = _ffn(h2, x1, modsel[layer], ffn_w_gate[i].astype(BF16), ffn_w_up[i].astype(BF16),
                      ffn_w_down[i].astype(BF16))
        else:
            etab, ctab = _moe_tables(cnt)
            ys = _moe_experts(etab, h2, rk[:, :N_EXPERTS].T, moe_w_gate[i:i + 1], moe_w_up[i:i + 1],
                              moe_w_down[i:i + 1])
            xs = _moe_combine(ctab, x1, gate, rk, modsel[layer, 1, 5][None, :], ys)
    return xs[None]
```

```python
import functools

import numpy as np
import jax
import jax.numpy as jnp
from jax import lax
from jax.experimental import pallas as pl
from jax.experimental.pallas import tpu as pltpu

F32 = jnp.float32
BF16 = jnp.bfloat16

D = 1024
SEQ = 16384
CTX = 256
ROWS = CTX + SEQ
DEPTH = 2
GRID_W = 64
EPS = 1e-6
HEAD_DIM = 64
N_Q_HEADS = 8
N_KV_HEADS = 2
ATTN_W = 512
KV_W = 128
WINDOW = 128
ROPE_THETA = 10000.0
HG_HEADS = 4
HG_K = 64
HG_W = 256
GM_GROUPS = 4
GM_DIM = 64
GM_W = 256
GM_CHUNK = 128
IN_W = 2560
FFN_DIM = 2816
N_EXPERTS = 8
EXPERT_DIM = 3584

VMEM_LIMIT = 56 * 1024 * 1024
NEG = -1e30
LOG2E = 1.4426950408889634

TM = 256
HT = 128
HG_SUB = 2
HG_LEVELS = (8, 16, 32, 64)
AB = 128
ASUB = 4
FFN_TM = 640
FFN_FC = 1408
MOE_TILE = 256
MOE_CAP = 17 * MOE_TILE
MOE_ITEMS = 2 * SEQ // MOE_CAP + N_EXPERTS
MOE_GC = 256
MOE_GT = 128
MOE_GB = 2048
MOE_NG = SEQ // MOE_GB
MOE_FC = 512
MOE_NF = EXPERT_DIM // MOE_FC
MOE_CW = 256
MOE_CT = 128
MOE_CS = MOE_CW // MOE_CT + 1
MOE_VMEM_LIMIT = 62 * 1024 * 1024


def _params(sem, vmem=VMEM_LIMIT):
    return pltpu.CompilerParams(dimension_semantics=sem, vmem_limit_bytes=vmem)


def _block_ones(n, seg):
    i = np.arange(n)
    return (i[:, None] // seg == i[None, :] // seg).astype(np.float32)


def _ada_kernel(a_ref, w_ref, b_ref, o_ref):
    a = a_ref[...]
    s = (a * jax.nn.sigmoid(a)).astype(BF16)
    w = w_ref[0].astype(BF16)
    o_ref[0] = jnp.dot(s, w, preferred_element_type=F32) + b_ref[0]


def _ada(cc, w_ada, b_ada):
    tn = 1536
    return pl.pallas_call(
        _ada_kernel,
        grid=(DEPTH, 6 * D // tn),
        in_specs=[
            pl.BlockSpec((8, D), lambda l, n: (0, 0)),
            pl.BlockSpec((1, D, tn), lambda l, n: (l, 0, n)),
            pl.BlockSpec((1, 1, tn), lambda l, n: (l, 0, n)),
        ],
        out_specs=pl.BlockSpec((1, 8, tn), lambda l, n: (l, 0, n)),
        out_shape=jax.ShapeDtypeStruct((DEPTH, 8, 6 * D), F32),
        compiler_params=_params(("parallel", "parallel")),
        name="ada_mod",
    )(cc, w_ada, b_ada.reshape(DEPTH, 1, 6 * D))


def _gelu(x):
    return 0.5 * x * (1.0 + lax.erf(x * 0.7071067811865476))


def _load_rows(refs, split):
    if not split:
        return refs[0][...], refs[1:]
    return jnp.where(pl.program_id(0) == 0, refs[0][...], refs[1][...]), refs[2:]


def _row_specs(xs, width, off=0):
    if not isinstance(xs, tuple):
        return [pl.BlockSpec((TM, width), lambda i: (i + off, 0))], [xs]
    assert off == 0 and CTX == TM
    return [pl.BlockSpec((TM, width), lambda i: (0, 0)),
            pl.BlockSpec((TM, width), lambda i: (jnp.maximum(i - 1, 0), 0))], list(xs)


def _inproj_kernel(*refs, split):
    x, refs = _load_rows(refs, split)
    (mod_ref, w_ref, cs_ref, sn_ref, gain_ref, bq_ref, bg_ref, gng_ref, ws_ref, bs_ref,
     q_ref, k_ref, v_ref, ph_ref, gm_ref, wb_ref) = refs

    @pl.when(pl.program_id(0) == 0)
    def _():
        wb_ref[...] = w_ref[0].astype(BF16)

    r = lax.rsqrt(jnp.mean(x * x, axis=-1, keepdims=True) + EPS)
    mod = mod_ref[0]
    h = (x * r) * (1.0 + mod[1:2]) + mod[0:1]
    p = jnp.dot(h.astype(BF16), wb_ref[...], preferred_element_type=F32)

    qk = p[:, :ATTN_W + KV_W]
    ss = jnp.dot((qk * qk).astype(BF16), bq_ref[...], preferred_element_type=F32)
    qkn = qk * lax.rsqrt(ss * (1.0 / HEAD_DIM) + EPS) * gain_ref[...]
    lane = lax.broadcasted_iota(jnp.int32, (TM, 128), 1)
    first_half = (lane // 16) % 2 == 0
    cs = cs_ref[...]
    sn = sn_ref[...]
    rot = []
    for j in range(5):
        xs = qkn[:, 128 * j:128 * (j + 1)]
        partner = jnp.where(first_half, pltpu.roll(xs, 112, 1), pltpu.roll(xs, 16, 1))
        rot.append(xs * cs + partner * sn)
    q_ref[...] = jnp.concatenate(rot[:4], axis=1).astype(BF16)

    low = lane < 64

    def dup(t):
        sw = pltpu.roll(t, 64, 1)
        return jnp.concatenate([jnp.where(low, t, sw), jnp.where(low, sw, t)], axis=1)

    k_ref[...] = dup(rot[4]).astype(BF16)
    v_ref[...] = dup(p[:, ATTN_W + KV_W:ATTN_W + 2 * KV_W]).astype(BF16)

    ph_ref[...] = p[:, 768:2048]

    u = _gelu(p[:, 2048:2304])
    vv = _gelu(p[:, 2304:2560])
    ssv = jnp.dot((vv * vv).astype(BF16), bg_ref[...], preferred_element_type=F32)
    vn = vv * lax.rsqrt(ssv * (1.0 / GM_DIM) + EPS) * gng_ref[...]
    group = lax.broadcasted_iota(jnp.int32, (GM_CHUNK, GM_W), 1) // GM_DIM
    for c in range(TM // GM_CHUNK):
        vc = vn[GM_CHUNK * c:GM_CHUNK * (c + 1)]
        acc = bs_ref[...]
        for g in range(GM_GROUPS):
            vg = jnp.where(group == g, vc, 0.0).astype(BF16)
            acc = acc + jnp.dot(ws_ref[g], vg, preferred_element_type=F32)
        gm_ref[GM_CHUNK * c:GM_CHUNK * (c + 1), :] = (u[GM_CHUNK * c:GM_CHUNK * (c + 1)] * acc).astype(BF16)


def _inproj(xs, modsel, w_in, layer, cs, sn, gain, bq, bg, gng, ws, bs):
    nt = ROWS // TM
    const2 = lambda i: (0, 0)
    x_specs, x_args = _row_specs(xs, D)
    return pl.pallas_call(
        functools.partial(_inproj_kernel, split=isinstance(xs, tuple)),
        grid=(nt,),
        in_specs=x_specs + [
            pl.BlockSpec((1, 8, D), lambda i: (jnp.minimum(i, 1), 0, 0)),
            pl.BlockSpec((1, D, IN_W), lambda i: (layer, 0, 0), pipeline_mode=pl.Buffered(1)),
            pl.BlockSpec((TM, 128), lambda i: (i, 0)),
            pl.BlockSpec((TM, 128), lambda i: (i, 0)),
            pl.BlockSpec((1, ATTN_W + KV_W), const2),
            pl.BlockSpec((ATTN_W + KV_W, ATTN_W + KV_W), const2),
            pl.BlockSpec((GM_W, GM_W), const2),
            pl.BlockSpec((1, GM_W), const2),
            pl.BlockSpec((GM_GROUPS, GM_CHUNK, GM_CHUNK), lambda i: (0, 0, 0)),
            pl.BlockSpec((GM_CHUNK, GM_W), const2),
        ],
        out_specs=[
            pl.BlockSpec((TM, ATTN_W), lambda i: (i, 0)),
            pl.BlockSpec((TM, 2 * KV_W), lambda i: (i, 0)),
            pl.BlockSpec((TM, 2 * KV_W), lambda i: (i, 0)),
            pl.BlockSpec((TM, 5 * HG_W), lambda i: (i, 0)),
            pl.BlockSpec((TM, GM_W), lambda i: (i, 0)),
        ],
        out_shape=[
            jax.ShapeDtypeStruct((ROWS, ATTN_W), BF16),
            jax.ShapeDtypeStruct((ROWS, 2 * KV_W), BF16),
            jax.ShapeDtypeStruct((ROWS, 2 * KV_W), BF16),
            jax.ShapeDtypeStruct((ROWS, 5 * HG_W), F32),
            jax.ShapeDtypeStruct((ROWS, GM_W), BF16),
        ],
        scratch_shapes=[pltpu.VMEM((D, IN_W), BF16)],
        compiler_params=_params(("arbitrary",)),
        name="inproj",
    )(*x_args, modsel, w_in, cs, sn, gain, bq, bg, gng, ws, bs)


def _hgrn_consts():
    t = np.arange(HT)
    tt, ss = t[:, None], t[None, :]
    stack, masks = [], []
    for m in HG_LEVELS:
        same = (tt // m) == (ss // m)
        stack += [same & (ss <= tt), same & (ss > tt)]
        masks.append(((tt // (2 * m)) == (ss // (2 * m))) & ((tt % (2 * m)) >= m) & ((ss % (2 * m)) < m))
    stack.append(ss <= tt)
    masks.append(((tt // 8) == (ss // 8)) & (ss <= tt))
    lst_f = np.tile(np.concatenate(stack, axis=0).astype(np.float32), (1, 2))
    lst_b = np.tile(np.concatenate([b.T for b in stack], axis=0).astype(np.float32), (1, 2))
    msk_f = np.stack([np.tile(m, (1, HG_HEADS)) for m in masks]).astype(np.float32)
    msk_b = np.stack([np.tile(m.T, (1, HG_HEADS)) for m in masks]).astype(np.float32)
    e = np.zeros((8, HG_W, 128), np.float32)
    tl = np.zeros((128, HG_HEADS * HT), np.float32)
    for h in range(HG_HEADS):
        for j in range(8):
            e[j, h * HG_K:(h + 1) * HG_K, h * 8 + j] = 1.0
            tl[h * 8 + j, h * HT + t[(t % 8) == j]] = 1.0
    lane_head = np.arange(HG_W) // HG_K
    hm = np.stack([np.tile((lane_head == h).astype(np.float32), (HT, 1)) for h in range(HG_HEADS)])
    return lst_f, lst_b, msk_f, msk_b, e.reshape(8 * HG_W, 128), tl, _block_ones(HG_W, HG_K), hm


def _hgrn_direction(zq, zf, vi, lb, lst_ref, msk_ref, e_ref, tl_ref, bd_ref, hm_ref, st_ref, q3_ref, k3_ref, p3_ref):
    q = (zq * jax.nn.sigmoid(zq)) * (HG_K ** -0.5)
    f = lb + (1.0 - lb) * jax.nn.sigmoid(zf)
    g = jnp.log(f) * LOG2E
    kk = 1.0 - f
    g_hi = g.astype(BF16)
    g_lo = (g - g_hi.astype(F32)).astype(BF16)
    cum = jnp.dot(lst_ref[...], jnp.concatenate([g_hi, g_lo], axis=0), preferred_element_type=F32)
    nl = len(HG_LEVELS)
    c = cum[2 * nl * HT:(2 * nl + 1) * HT]
    ctot = jnp.sum(g, axis=0, keepdims=True)
    nt_dims = (((1,), (1,)), ((), ()))

    def per_head_rows(t):
        tb = t.astype(BF16)
        return jnp.concatenate([tb * hm_ref[h] for h in range(HG_HEADS)], axis=0)

    yield
    st = st_ref[...]
    qe = (q * jnp.exp2(c)).astype(BF16)
    o = lax.dot_general(qe, st.astype(BF16), nt_dims, preferred_element_type=F32)
    ke = (kk * jnp.exp2(ctot - c)).astype(BF16)
    upd = lax.dot_general(vi.astype(BF16), ke, (((0,), (0,)), ((), ())), preferred_element_type=F32)
    st_ref[...] = st * jnp.exp2(ctot) + upd * bd_ref[...]
    yield

    p8 = cum[0:HT]
    q3_ref[...] = q.reshape(HT // 8, 8, HG_W)
    k3_ref[...] = kk.reshape(HT // 8, 8, HG_W)
    p3_ref[...] = p8.reshape(HT // 8, 8, HG_W)
    slabs = []
    for j in range(8):
        kj = k3_ref[:, j:j + 1, :]
        pj = p3_ref[:, j:j + 1, :]
        slab = q3_ref[...] * kj * jnp.exp2(jnp.minimum(p3_ref[...] - pj, 0.0))
        slabs.append(slab.reshape(HT, HG_W).astype(BF16))
        if j % 2 == 1:
            yield
    a = jnp.dot(jnp.concatenate(slabs, axis=1), e_ref[...], preferred_element_type=F32)
    sc = msk_ref[nl] * jnp.dot(a.astype(BF16), tl_ref[...], preferred_element_type=F32)

    for li in range(nl):
        yield
        pm = cum[2 * li * HT:(2 * li + 1) * HT]
        rm = cum[(2 * li + 1) * HT:(2 * li + 2) * HT]
        qt = (q * jnp.exp2(pm)).astype(BF16)
        kt = per_head_rows(kk * jnp.exp2(rm))
        sc = sc + msk_ref[li] * lax.dot_general(qt, kt, nt_dims, preferred_element_type=F32)

    yield
    return o + jnp.dot(sc.astype(BF16), per_head_rows(vi), preferred_element_type=F32)


def _hgrn_kernel(qf_ref, ff_ref, if_ref, qb_ref, fb_ref, ib_ref, lbf_ref, lbb_ref,
                 lstf_ref, lstb_ref, mskf_ref, mskb_ref, e_ref, tl_ref, bd_ref, hm_ref,
                 of_ref, ob_ref, stf_ref, stb_ref, *scr):
    @pl.when(pl.program_id(0) == 0)
    def _():
        stf_ref[...] = jnp.zeros_like(stf_ref)
        stb_ref[...] = jnp.zeros_like(stb_ref)

    scans, dests = [], []
    for k in range(HG_SUB):
        rf = slice(k * HT, (k + 1) * HT)
        rb = slice((HG_SUB - 1 - k) * HT, (HG_SUB - k) * HT)
        s3 = scr[6 * k:6 * (k + 1)]
        scans.append(_hgrn_direction(qf_ref[rf, :], ff_ref[rf, :], if_ref[rf, :], lbf_ref[...], lstf_ref, mskf_ref,
                                     e_ref, tl_ref, bd_ref, hm_ref, stf_ref, *s3[:3]))
        dests.append((of_ref, rf))
        scans.append(_hgrn_direction(qb_ref[rb, :], fb_ref[rb, :], ib_ref[rb, :], lbb_ref[...], lstb_ref, mskb_ref,
                                     e_ref, tl_ref, bd_ref, hm_ref, stb_ref, *s3[3:]))
        dests.append((ob_ref, rb))
    live = list(range(len(scans)))
    while live:
        for d in list(live):
            try:
                next(scans[d])
            except StopIteration as done:
                ref, rows = dests[d]
                ref[rows, :] = done.value
                live.remove(d)


def _hgrn(ph, lbf, lbb, consts):
    lst_f, lst_b, msk_f, msk_b, e, tl, bd, hm = consts
    rows = HG_SUB * HT
    nt = ROWS // rows
    nc = CTX // rows

    def fwd(col):
        return pl.BlockSpec((rows, HG_W), lambda j: (j, col))

    def bwd(col):
        return pl.BlockSpec((rows, HG_W), lambda j: (jnp.where(j < nc, nc - 1 - j, nt + nc - 1 - j), col))

    def const(a):
        nd = a.ndim
        return pl.BlockSpec(a.shape, lambda j: (0,) * nd)

    scr3 = pltpu.VMEM((HT // 8, 8, HG_W), F32)
    return pl.pallas_call(
        _hgrn_kernel,
        grid=(nt,),
        in_specs=[fwd(0), fwd(1), fwd(3), bwd(0), bwd(2), bwd(3), const(lbf), const(lbb),
                  const(lst_f), const(lst_b), const(msk_f), const(msk_b), const(e), const(tl), const(bd), const(hm)],
        out_specs=[pl.BlockSpec((rows, HG_W), lambda j: (j, 0)),
                   pl.BlockSpec((rows, HG_W), lambda j: (jnp.where(j < nc, nc - 1 - j, nt + nc - 1 - j), 0))],
        out_shape=[jax.ShapeDtypeStruct((ROWS, HG_W), F32), jax.ShapeDtypeStruct((ROWS, HG_W), F32)],
        scratch_shapes=[pltpu.VMEM((HG_W, HG_W), F32), pltpu.VMEM((HG_W, HG_W), F32)] + [scr3] * (6 * HG_SUB),
        compiler_params=_params(("arbitrary",)),
        name="hgrn_scan",
    )(ph, ph, ph, ph, ph, ph, lbf, lbb, lst_f, lst_b, msk_f, msk_b, e, tl, bd, hm)


def _attn_core(q, kcat, vcat, bias, sink_ref, hm_ref, o_ref):
    nk = kcat.shape[0]
    nt_dims = (((1,), (1,)), ((), ()))
    low = lax.broadcasted_iota(jnp.int32, (AB, 128), 1) < HEAD_DIM
    den_lane = (HEAD_DIM, 0)
    scores, vsels = [], []
    for g in range(N_KV_HEADS):
        kg = kcat[:, 128 * g:128 * (g + 1)]
        vg = vcat[:, 128 * g:128 * (g + 1)]
        vsels.append((vg * hm_ref[0, :nk, :] + hm_ref[2, :nk, :], vg * hm_ref[1, :nk, :] + hm_ref[3, :nk, :]))
        qs = []
        for pr in (2 * g, 2 * g + 1):
            qp = q[:, 128 * pr:128 * (pr + 1)]
            qs += [qp * hm_ref[0, :AB, :], qp * hm_ref[1, :AB, :]]
        scores.append(lax.dot_general(jnp.concatenate(qs, axis=0), kg, nt_dims, preferred_element_type=F32))
    halves = {}
    for hh in range(N_Q_HEADS // N_KV_HEADS):
        for g in range(N_KV_HEADS):
            a = hh % 2
            sk = sink_ref[4 * g + hh] * LOG2E
            sh = scores[g][AB * hh:AB * (hh + 1)]
            if bias is not None:
                nb = bias.shape[1]
                sh = jnp.concatenate([sh[:, :nb] + bias, sh[:, nb:]], axis=1)
            m = jnp.maximum(jnp.max(sh, axis=1, keepdims=True), sk)
            e = jnp.exp2(sh - m).astype(BF16)
            pv = jnp.dot(e, vsels[g][a], preferred_element_type=F32)
            den = pv[:, den_lane[a]:den_lane[a] + 1] + jnp.exp2(sk - m)
            halves[4 * g + hh] = pv * (1.0 / den)
    pairs = [jnp.where(low, halves[2 * p], halves[2 * p + 1]) for p in range(N_Q_HEADS // 2)]
    o_ref[...] = jnp.concatenate(pairs, axis=1).astype(BF16)


def _attn_lat_kernel(*refs):
    q_refs, refs = refs[:ASUB], refs[ASUB:]
    k_refs, kc_ref = refs[:ASUB + 2], refs[ASUB + 2]
    v_refs, vc_ref = refs[ASUB + 3:2 * ASUB + 5], refs[2 * ASUB + 5]
    sink_ref, hm_ref, o_ref = refs[2 * ASUB + 6:]
    last = pl.num_programs(0) * ASUB - 1
    a = lax.broadcasted_iota(jnp.int32, (AB, 3 * AB), 0)
    r = lax.broadcasted_iota(jnp.int32, (AB, 3 * AB), 1)
    for u in range(ASUB):
        blk = pl.program_id(0) * ASUB + u
        kcat = jnp.concatenate([k_refs[u][...], k_refs[u + 1][...], k_refs[u + 2][...], kc_ref[...]], axis=0)
        vcat = jnp.concatenate([v_refs[u][...], v_refs[u + 1][...], v_refs[u + 2][...], vc_ref[...]], axis=0)
        lo = jnp.where(blk == 0, AB, 0)
        hi = jnp.where(blk == last, 2 * AB - 1, 3 * AB - 1)
        lower = jnp.maximum(a + (AB - WINDOW), lo)
        upper = jnp.minimum(a + (AB + WINDOW), hi)
        bias = jnp.where(r >= lower, jnp.where(r <= upper, 0.0, NEG), NEG)
        rows = slice(u * AB, (u + 1) * AB)
        _attn_core(q_refs[u][...], kcat, vcat, bias, sink_ref, hm_ref, o_ref.at[rows, :])


def _attn_ctx_kernel(q_ref, kc_ref, vc_ref, sink_ref, hm_ref, o_ref):
    _attn_core(q_ref[...], kc_ref[...], vc_ref[...], None, sink_ref, hm_ref, o_ref)


def _head_masks():
    hm = np.zeros((4, 3 * AB + CTX, 128), np.float32)
    hm[0, :, :HEAD_DIM] = 1.0
    hm[1, :, HEAD_DIM:] = 1.0
    hm[2, :, HEAD_DIM] = 1.0
    hm[3, :, 0] = 1.0
    return hm


def _attn_lat(q, k, v, sink, hm):
    rows = ASUB * AB
    nb = SEQ // rows
    off = CTX // AB
    last = ROWS // AB - 1
    kv = [pl.BlockSpec((AB, 2 * KV_W), (lambda i, j=j: (jnp.minimum(i * ASUB + off - 1 + j, last), 0)))
          for j in range(ASUB + 2)]
    ctx = pl.BlockSpec((CTX, 2 * KV_W), lambda i: (0, 0))
    return pl.pallas_call(
        _attn_lat_kernel,
        grid=(nb,),
        in_specs=[pl.BlockSpec((AB, ATTN_W), (lambda i, u=u: (i * ASUB + off + u, 0))) for u in range(ASUB)]
        + kv + [ctx] + kv + [ctx]
        + [pl.BlockSpec(memory_space=pltpu.SMEM), pl.BlockSpec((4, 3 * AB + CTX, 128), lambda i: (0, 0, 0))],
        out_specs=pl.BlockSpec((rows, ATTN_W), lambda i: (i, 0)),
        out_shape=jax.ShapeDtypeStruct((SEQ, ATTN_W), BF16),
        compiler_params=_params(("parallel",)),
        name="attn_window",
    )(*([q] * ASUB), *([k] * (ASUB + 2)), k, *([v] * (ASUB + 2)), v, sink, hm)


def _attn_ctx(q, k, v, sink, hm):
    ctx = pl.BlockSpec((CTX, 2 * KV_W), lambda i: (0, 0))
    return pl.pallas_call(
        _attn_ctx_kernel,
        grid=(CTX // AB,),
        in_specs=[pl.BlockSpec((AB, ATTN_W), lambda i: (i, 0)), ctx, ctx,
                  pl.BlockSpec(memory_space=pltpu.SMEM),
                  pl.BlockSpec((4, 3 * AB + CTX, 128), lambda i: (0, 0, 0))],
        out_specs=pl.BlockSpec((AB, ATTN_W), lambda i: (i, 0)),
        out_shape=jax.ShapeDtypeStruct((CTX, ATTN_W), BF16),
        compiler_params=_params(("parallel",)),
        name="attn_context",
    )(q, k, v, sink, hm)


def _outproj_kernel(*refs, with_ctx, with_router, split):
    x_in, refs = _load_rows(refs, split)
    it = iter(refs)
    mod_ref = next(it)
    al_ref = next(it)
    ac_ref = next(it) if with_ctx else None
    of_ref, ob_ref, hg_ref, gm_ref, w_ref, hgain_ref, bh_ref = (next(it) for _ in range(7))
    rt_ref = next(it) if with_router else None
    lt_ref = next(it) if with_router else None
    x1_ref = next(it)
    h2_ref = next(it)
    if with_router:
        gate_ref, rank_ref, cnt_ref, base_ref = (next(it) for _ in range(4))
    wb_ref = next(it)

    @pl.when(pl.program_id(0) == 0)
    def _():
        wb_ref[...] = w_ref[0].astype(BF16)

    mod = mod_ref[0]
    attn = al_ref[...]
    if with_ctx:
        attn = jnp.where(pl.program_id(0) == 0, ac_ref[...], attn)
    o = of_ref[...] + ob_ref[...]
    ss = jnp.dot((o * o).astype(BF16), bh_ref[...], preferred_element_type=F32)
    hg = hg_ref[...]
    y = o * lax.rsqrt(ss * (1.0 / HG_K) + EPS) * hgain_ref[...] * (hg * jax.nn.sigmoid(hg))
    mix = jnp.concatenate([attn, y.astype(BF16), gm_ref[...]], axis=1)
    proj = jnp.dot(mix, wb_ref[...], preferred_element_type=F32)
    x1 = x_in + mod[2:3] * proj
    x1_ref[...] = x1
    r = lax.rsqrt(jnp.mean(x1 * x1, axis=-1, keepdims=True) + EPS)
    h2 = (x1 * r) * (1.0 + mod[4:5]) + mod[3:4]
    h2_hi = h2.astype(BF16)
    h2_ref[...] = h2_hi

    if with_router:
        h2_lo = (h2 - h2_hi.astype(F32)).astype(BF16)
        two = jnp.dot(h2_hi, rt_ref[...], preferred_element_type=F32)
        logits = two[:, :128] + two[:, 128:] + jnp.dot(h2_lo, rt_ref[:, :128], preferred_element_type=F32)
        lane = lax.broadcasted_iota(jnp.int32, logits.shape, 1)
        l1 = jnp.where(lane < N_EXPERTS, logits, NEG)
        m1 = jnp.max(l1, axis=1, keepdims=True)
        i1 = jnp.min(jnp.where(l1 == m1, lane, 128), axis=1, keepdims=True)
        l2 = jnp.where(lane == i1, NEG, l1)
        m2 = jnp.max(l2, axis=1, keepdims=True)
        i2 = jnp.min(jnp.where(l2 == m2, lane, 128), axis=1, keepdims=True)
        e2 = jnp.exp(m2 - m1)
        w1 = 1.0 / (1.0 + e2)
        gate_ref[...] = jnp.where(lane == i1, w1, 0.0) + jnp.where(lane == i2, e2 * w1, 0.0)

        @pl.when(pl.program_id(0) == 0)
        def _():
            base_ref[...] = jnp.zeros_like(base_ref)

        mem = jnp.where(lane == i1, 1.0, jnp.where(lane == i2, 1.0, 0.0))
        local = jnp.dot(lt_ref[...], mem.astype(BF16), preferred_element_type=F32)
        base = base_ref[...]
        rank_ref[...] = jnp.where(mem > 0.0, base + local, -1.0)
        after = base + jnp.sum(mem, axis=0, keepdims=True)
        cnt_ref[0, 0:1, :] = base
        cnt_ref[0, 1:2, :] = after
        cnt_ref[0, 2:8, :] = jnp.zeros((6, 128), F32)
        base_ref[...] = after


def _outproj(xs, modsel, attn_lat, attn_ctx, o_f, o_b, ph, gm, w_out, layer, hgain, bh, router, *, with_ctx):
    with_router = router is not None
    off = 0 if with_ctx else CTX // TM
    nt = ROWS // TM - off
    rows_out = ROWS if with_ctx else SEQ
    row = lambda w: pl.BlockSpec((TM, w), lambda i: (i + off, 0))
    const2 = lambda i: (0, 0)
    x_specs, x_args = _row_specs(xs, D, off)
    in_specs = x_specs + [
        pl.BlockSpec((1, 8, D), lambda i: (jnp.minimum(i + off, 1), 0, 0)),
        pl.BlockSpec((TM, ATTN_W), lambda i: (jnp.maximum(i + off - CTX // TM, 0), 0))]
    args = x_args + [modsel, attn_lat]
    if with_ctx:
        in_specs.append(pl.BlockSpec((TM, ATTN_W), const2))
        args.append(attn_ctx)
    in_specs += [row(HG_W), row(HG_W),
                 pl.BlockSpec((TM, HG_W), lambda i: (i + off, 4)),
                 row(GM_W),
                 pl.BlockSpec((1, D, D), lambda i: (layer, 0, 0), pipeline_mode=pl.Buffered(1)),
                 pl.BlockSpec((1, HG_W), const2),
                 pl.BlockSpec((HG_W, HG_W), const2)]
    args += [o_f, o_b, ph, gm, w_out, hgain, bh]
    out_specs = [pl.BlockSpec((TM, D), lambda i: (i, 0)), pl.BlockSpec((TM, D), lambda i: (i, 0))]
    out_shape = [jax.ShapeDtypeStruct((rows_out, D), F32), jax.ShapeDtypeStruct((rows_out, D), BF16)]
    scratch = []
    if with_router:
        t = np.arange(TM)
        strict_lower = jnp.asarray((t[None, :] < t[:, None]).astype(np.float32), BF16)
        in_specs += [pl.BlockSpec((D, 256), const2), pl.BlockSpec((TM, TM), const2)]
        args += [router, strict_lower]
        out_specs += [pl.BlockSpec((TM, 128), lambda i: (i, 0)), pl.BlockSpec((TM, 128), lambda i: (i, 0)),
                      pl.BlockSpec((1, 8, 128), lambda i: (i, 0, 0))]
        out_shape += [jax.ShapeDtypeStruct((rows_out, 128), F32), jax.ShapeDtypeStruct((rows_out, 128), F32),
                      jax.ShapeDtypeStruct((nt, 8, 128), F32)]
        scratch = [pltpu.VMEM((1, 128), F32)]
    scratch.append(pltpu.VMEM((D, D), BF16))
    return pl.pallas_call(
        functools.partial(_outproj_kernel, with_ctx=with_ctx, with_router=with_router, split=isinstance(xs, tuple)),
        grid=(nt,),
        in_specs=in_specs,
        out_specs=out_specs,
        out_shape=out_shape,
        scratch_shapes=scratch,
        compiler_params=_params(("arbitrary",)),
        name="outproj",
    )(*args)


def _ffn_kernel(h_ref, x_ref, mod_ref, wg_ref, wu_ref, wd_ref, o_ref):
    h = h_ref[...]
    acc = None
    for f in range(FFN_DIM // FFN_FC):
        cols = slice(f * FFN_FC, (f + 1) * FFN_FC)
        a = jnp.dot(h, wg_ref[:, cols], preferred_element_type=F32)
        b = jnp.dot(h, wu_ref[:, cols], preferred_element_type=F32)
        t = (a * jax.nn.sigmoid(a) * b).astype(BF16)
        y = jnp.dot(t, wd_ref[cols, :], preferred_element_type=F32)
        acc = y if acc is None else acc + y
    row = pl.program_id(0) * FFN_TM + lax.broadcasted_iota(jnp.int32, (FFN_TM, D), 0)
    gate = jnp.where(row < CTX, mod_ref[0][5:6], mod_ref[1][5:6])
    o_ref[...] = x_ref[...] + gate * acc


def _ffn(h2, x1, modsel, wg, wu, wd):
    nt = ROWS // FFN_TM
    resident = dict(pipeline_mode=pl.Buffered(1))
    return pl.pallas_call(
        _ffn_kernel,
        grid=(nt,),
        in_specs=[pl.BlockSpec((FFN_TM, D), lambda i: (i, 0)),
                  pl.BlockSpec((FFN_TM, D), lambda i: (i, 0)),
                  pl.BlockSpec((2, 8, D), lambda i: (0, 0, 0)),
                  pl.BlockSpec((D, FFN_DIM), lambda i: (0, 0), **resident),
                  pl.BlockSpec((D, FFN_DIM), lambda i: (0, 0), **resident),
                  pl.BlockSpec((FFN_DIM, D), lambda i: (0, 0), **resident)],
        out_specs=pl.BlockSpec((FFN_TM, D), lambda i: (i, 0)),
        out_shape=jax.ShapeDtypeStruct((ROWS, D), F32),
        compiler_params=_params(("parallel",)),
        name="ffn_dense",
    )(h2, x1, modsel, wg, wu, wd)


def _moe_index(i, s, ie, ibase, irows, cnt):
    valid = irows[i] > 0
    chunk = jnp.where(valid, jnp.minimum(s, MOE_NG - 1), MOE_NG - 1)
    fidx = jnp.where(valid, jnp.maximum(s - MOE_NG, 0), MOE_NF - 1)
    return chunk, fidx


def _moe_expert_kernel(ie_ref, ibase_ref, irows_ref, cnt_ref,
                       h_ref, rk_ref, wg_ref, wu_ref, wd_ref, ys_ref, yg_ref):
    i = pl.program_id(0)
    s = pl.program_id(1)
    e = ie_ref[i]
    base = ibase_ref[i]
    rows = irows_ref[i]
    ntile = (rows + MOE_TILE - 1) // MOE_TILE

    @pl.when(s == 0)
    def _():
        ys_ref[...] = jnp.zeros_like(ys_ref)

    def gather_chunk(j):
        c = s * (MOE_GB // MOE_GC) + j
        cols = slice(j * MOE_GC, (j + 1) * MOE_GC)
        lo = jnp.maximum(cnt_ref[c * N_EXPERTS + e] - base, 0)
        hi = jnp.minimum(cnt_ref[(c + 1) * N_EXPERTS + e] - base, rows) - 1
        a_lo = lo // MOE_GT
        a_hi = jnp.where(hi >= lo, hi // MOE_GT, a_lo - 1)
        rel = rk_ref[pl.ds(e, 1), cols].astype(jnp.int32) - base
        h = h_ref[cols, :]

        def tile_body(a, carry):
            off = pl.multiple_of(a * MOE_GT, MOE_GT)
            slot = lax.broadcasted_iota(jnp.int32, (MOE_GT, MOE_GC), 0) + off
            onehot = jnp.where(rel == slot, 1.0, 0.0).astype(BF16)
            packed = jnp.dot(onehot, h, preferred_element_type=F32)
            ys_ref[pl.ds(off, MOE_GT), :] = (ys_ref[pl.ds(off, MOE_GT), :].astype(F32) + packed).astype(BF16)
            return carry

        lax.fori_loop(a_lo, a_hi + 1, tile_body, 0)

    @pl.when((s < MOE_NG) & (rows > 0))
    def _():
        for j in range(MOE_GB // MOE_GC):
            gather_chunk(j)

    def expert_step(mode):
        wg = wg_ref[0, 0].astype(BF16)
        wu = wu_ref[0, 0].astype(BF16)
        wd = wd_ref[0, 0].astype(BF16)

        def tile_body(k, carry):
            off = pl.multiple_of(k * MOE_TILE, MOE_TILE)
            xg = ys_ref[pl.ds(off, MOE_TILE), :]
            a = jnp.dot(xg, wg, preferred_element_type=F32)
            b = jnp.dot(xg, wu, preferred_element_type=F32)
            y = jnp.dot((a * jax.nn.sigmoid(a) * b).astype(BF16), wd, preferred_element_type=F32)
            if mode == "first":
                yg_ref[pl.ds(off, MOE_TILE), :] = y
            elif mode == "middle":
                yg_ref[pl.ds(off, MOE_TILE), :] += y
            else:
                ys_ref[pl.ds(off, MOE_TILE), :] = (yg_ref[pl.ds(off, MOE_TILE), :] + y).astype(BF16)
            return carry

        lax.fori_loop(0, ntile, tile_body, 0)

    f = s - MOE_NG
    pl.when((f == 0) & (rows > 0))(functools.partial(expert_step, "first"))
    pl.when((f > 0) & (f < MOE_NF - 1) & (rows > 0))(functools.partial(expert_step, "middle"))
    pl.when((f == MOE_NF - 1) & (rows > 0))(functools.partial(expert_step, "last"))


def _moe_experts(tables, h2, rk_t, wg, wu, wd):
    ie, ibase, irows, cnt = tables

    def hmap(i, s, ie, ibase, irows, cnt):
        return (_moe_index(i, s, ie, ibase, irows, cnt)[0], 0)

    def rmap(i, s, ie, ibase, irows, cnt):
        return (0, _moe_index(i, s, ie, ibase, irows, cnt)[0])

    def upmap(i, s, ie, ibase, irows, cnt):
        return (0, ie[i], 0, _moe_index(i, s, ie, ibase, irows, cnt)[1])

    def downmap(i, s, ie, ibase, irows, cnt):
        return (0, ie[i], _moe_index(i, s, ie, ibase, irows, cnt)[1], 0)

    grid_spec = pltpu.PrefetchScalarGridSpec(
        num_scalar_prefetch=4,
        grid=(MOE_ITEMS, MOE_NG + MOE_NF),
        in_specs=[pl.BlockSpec((MOE_GB, D), hmap),
                  pl.BlockSpec((N_EXPERTS, MOE_GB), rmap),
                  pl.BlockSpec((1, 1, D, MOE_FC), upmap),
                  pl.BlockSpec((1, 1, D, MOE_FC), upmap),
                  pl.BlockSpec((1, 1, MOE_FC, D), downmap)],
        out_specs=pl.BlockSpec((MOE_CAP, D), lambda i, s, ie, ibase, irows, cnt: (i, 0)),
        scratch_shapes=[pltpu.VMEM((MOE_CAP, D), F32)],
    )
    return pl.pallas_call(
        _moe_expert_kernel,
        grid_spec=grid_spec,
        out_shape=jax.ShapeDtypeStruct((MOE_ITEMS * MOE_CAP, D), BF16),
        compiler_params=_params(("arbitrary", "arbitrary"), vmem=MOE_VMEM_LIMIT),
        name="moe_experts",
    )(ie, ibase, irows, cnt, h2, rk_t, wg, wu, wd)


def _moe_combine_kernel(tix_ref, se_ref, f0_ref, f1_ref, f2_ref, wide_ref, x_ref, gate_ref, rk_ref, gf_ref, *refs):
    y_refs, o_ref = refs[:-1], refs[-1]
    w = pl.program_id(0)

    def unpack(ntile):
        gate = gate_ref[...]
        rk = rk_ref[...]
        lane = lax.broadcasted_iota(jnp.int32, gate.shape, 1)
        slot = lax.broadcasted_iota(jnp.int32, (MOE_CW, ntile * MOE_CT), 1)
        acc = jnp.zeros((MOE_CW, D), F32)
        for e in range(N_EXPERTS):
            rcol = jnp.sum(jnp.where(lane == e, rk, 0.0), axis=1, keepdims=True)
            gcol = jnp.sum(jnp.where(lane == e, gate, 0.0), axis=1, keepdims=True)
            pos = rcol.astype(jnp.int32) + (se_ref[e] - tix_ref[w * N_EXPERTS + e] * MOE_CT)
            pos = jnp.where(rcol < 0.0, -1, pos)
            onehot = jnp.where(pos == slot, 1.0, 0.0).astype(BF16)
            ycat = jnp.concatenate([y_refs[MOE_CS * e + j][...] for j in range(ntile)], axis=0)
            acc = acc + gcol * jnp.dot(onehot, ycat, preferred_element_type=F32)
        o_ref[...] = x_ref[...] + gf_ref[...] * acc

    pl.when(wide_ref[w] == 0)(functools.partial(unpack, MOE_CS - 1))
    pl.when(wide_ref[w] != 0)(functools.partial(unpack, MOE_CS))


def _moe_combine(ctab, x1, gate, rk, gf, ys):
    assert MOE_CS == 3
    row = lambda wd_: pl.BlockSpec((MOE_CW, wd_), lambda w, *tabs: (w, 0))

    def ymap(e, j):
        return lambda w, *tabs: (tabs[2 + j][w * N_EXPERTS + e], 0)

    y_specs = [pl.BlockSpec((MOE_CT, D), ymap(e, j)) for e in range(N_EXPERTS) for j in range(MOE_CS)]
    grid_spec = pltpu.PrefetchScalarGridSpec(
        num_scalar_prefetch=len(ctab),
        grid=(SEQ // MOE_CW,),
        in_specs=[row(D), row(128), row(128), pl.BlockSpec((1, D), lambda w, *tabs: (0, 0))] + y_specs,
        out_specs=row(D),
    )
    return pl.pallas_call(
        _moe_combine_kernel,
        grid_spec=grid_spec,
        out_shape=jax.ShapeDtypeStruct((SEQ, D), F32),
        compiler_params=_params(("arbitrary",)),
        name="moe_combine",
    )(*ctab, x1, gate, rk, gf, *([ys] * (N_EXPERTS * MOE_CS)))


def _moe_tables(cnt):
    before = cnt[:, 0, :N_EXPERTS].astype(jnp.int32)
    total = cnt[-1, 1, :N_EXPERTS].astype(jnp.int32)
    nit = (total + MOE_CAP - 1) // MOE_CAP
    ends = jnp.cumsum(nit)
    first = ends - nit
    item = jnp.arange(MOE_ITEMS, dtype=jnp.int32)
    n_items = ends[-1]
    ie = jnp.minimum(jnp.sum((ends[None, :] <= item[:, None]).astype(jnp.int32), axis=1), N_EXPERTS - 1)
    last_e = ie[jnp.maximum(n_items - 1, 0)]
    valid = item < n_items
    ibase = jnp.where(valid, (item - first[ie]) * MOE_CAP, 0)
    irows = jnp.where(valid, jnp.clip(total[ie] - ibase, 0, MOE_CAP), 0)
    ie = jnp.where(valid, ie, last_e)
    per_chunk = MOE_GC // TM
    cnt_g = jnp.concatenate([before[::per_chunk], total[None, :]], axis=0).reshape(-1)
    se = first * MOE_CAP
    per_win = MOE_CW // TM
    win_before = before[::per_win]
    win_rows = jnp.concatenate([win_before[1:], total[None, :]], axis=0) - win_before
    start = se[None, :] + win_before
    tix = jnp.minimum(start // MOE_CT, n_items * (MOE_CAP // MOE_CT) - MOE_CS)
    tix = jnp.maximum(tix, 0)
    end = start - tix * MOE_CT + win_rows
    nw = tix.shape[0]
    earlier = (jnp.arange(nw)[None, :] <= jnp.arange(nw)[:, None])[:, :, None]
    fetch = []
    for j in range(MOE_CS):
        want = jnp.where(end > j * MOE_CT, tix + j, 0)
        fetch.append(jnp.max(jnp.where(earlier, want[None, :, :], 0), axis=1).reshape(-1).astype(jnp.int32))
    etab = (ie, ibase.astype(jnp.int32), irows.astype(jnp.int32), cnt_g)
    wide = jnp.max((end > (MOE_CS - 1) * MOE_CT).astype(jnp.int32), axis=1)
    ctab = (tix.reshape(-1).astype(jnp.int32), se.astype(jnp.int32)) + tuple(fetch) + (wide,)
    return etab, ctab


def _rope_tables():
    rows = SEQ // GRID_W
    row = np.repeat(np.arange(rows, dtype=np.float32), GRID_W)
    col = np.tile(np.arange(GRID_W, dtype=np.float32), rows)
    axis_dim = HEAD_DIM // 2
    inv_freq = (np.float32(ROPE_THETA) ** (-np.arange(0, axis_dim, 2, dtype=np.float32) / np.float32(axis_dim)))
    ang_r = row[:, None] * inv_freq.astype(np.float32)
    ang_c = col[:, None] * inv_freq.astype(np.float32)
    cos_r, sin_r, cos_c, sin_c = np.cos(ang_r), np.sin(ang_r), np.cos(ang_c), np.sin(ang_c)
    cs = np.concatenate([cos_r, cos_r, cos_c, cos_c], axis=1)
    sn = np.concatenate([-sin_r, sin_r, -sin_c, sin_c], axis=1)
    cs = np.concatenate([np.ones((CTX, HEAD_DIM), np.float32), cs], axis=0)
    sn = np.concatenate([np.zeros((CTX, HEAD_DIM), np.float32), sn], axis=0)
    return (jnp.asarray(np.tile(cs, (1, 2)).astype(np.float32)), jnp.asarray(np.tile(sn, (1, 2)).astype(np.float32)))


def kernel(x, c, ctx, c_ctx, w_ada, b_ada, w_in, w_out, q_norm_gain, k_norm_gain, attn_sink, hgrn_lower_bound, hgrn_out_gain, gmlp_w_s, gmlp_b_s, gmlp_norm_gain, ffn_w_gate, ffn_w_up, ffn_w_down, moe_router, moe_w_gate, moe_w_up, moe_w_down):
    cc = jnp.concatenate([c, c_ctx[None, :], jnp.zeros((6, D), F32)], axis=0)
    mod = _ada(cc, w_ada, b_ada)
    mod6 = mod[:, :2].reshape(DEPTH, 2, 6, D)
    modsel = jnp.pad(jnp.stack([mod6[:, 1], mod6[:, 0]], axis=1), ((0, 0), (0, 0), (0, 2), (0, 0)))

    lb_soft = jax.nn.softmax(hgrn_lower_bound.astype(F32), axis=1)
    lower = jnp.cumsum(lb_soft, axis=1) - lb_soft[:, :1]

    cs, sn = _rope_tables()
    hconsts = _hgrn_consts()
    hconsts = tuple(jnp.asarray(a, BF16) for a in hconsts[:2]) + tuple(jnp.asarray(a, F32) for a in hconsts[2:4]) \
        + tuple(jnp.asarray(a, BF16) for a in hconsts[4:6]) + (jnp.asarray(hconsts[6], F32),
                                                               jnp.asarray(hconsts[7], BF16))
    bq = jnp.asarray(_block_ones(ATTN_W + KV_W, HEAD_DIM), BF16)
    bg = jnp.asarray(_block_ones(GM_W, GM_DIM), BF16)
    hm = jnp.asarray(_head_masks(), BF16)

    xs = (ctx[0], x[0])
    for layer in range(DEPTH):
        last = layer == DEPTH - 1
        gain = jnp.concatenate([jnp.tile(q_norm_gain[layer], N_Q_HEADS) * (HEAD_DIM ** -0.5 * LOG2E),
                                jnp.tile(k_norm_gain[layer], N_KV_HEADS)])[None, :]
        bias = jnp.repeat(gmlp_b_s[layer].T, GM_DIM, axis=1)
        q, k, v, ph, gm = _inproj(xs, modsel[layer], w_in, layer, cs, sn, gain, bq, bg,
                                  gmlp_norm_gain[layer][None, :], gmlp_w_s[layer].astype(BF16), bias)
        o_f, o_b = _hgrn(ph, lower[0, layer][None, :], lower[1, layer][None, :], hconsts)
        sink = attn_sink[layer]
        attn_lat = _attn_lat(q, k, v, sink, hm)
        hgain = jnp.tile(hgrn_out_gain[layer], HG_HEADS)[None, :]
        if not last:
            attn_ctx = _attn_ctx(q, k, v, sink, hm)
            x1, h2 = _outproj(xs, modsel[layer], attn_lat, attn_ctx, o_f, o_b, ph, gm, w_out, layer, hgain, bg, None,
                              with_ctx=True)
        else:
            r32 = jnp.pad(moe_router[layer // 2], ((0, 0), (0, 128 - N_EXPERTS)))
            r_hi = r32.astype(BF16)
            router = jnp.concatenate([r_hi, (r32 - r_hi.astype(F32)).astype(BF16)], axis=1)
            x1, h2, gate, rk, cnt = _outproj(xs, modsel[layer], attn_lat, None, o_f, o_b, ph, gm, w_out, layer, hgain,
                                             bg, router, with_ctx=False)
        i = layer // 2
        if layer % 2 == 0:
            xs = _ffn(h2, x1, modsel[layer], ffn_w_gate[i].astype(BF16), ffn_w_up[i].astype(BF16),
                      ffn_w_down[i].astype(BF16))
        else:
            etab, ctab = _moe_tables(cnt)
            ys = _moe_experts(etab, h2, rk[:, :N_EXPERTS].T, moe_w_gate[i:i + 1], moe_w_up[i:i + 1],
                              moe_w_down[i:i + 1])
            xs = _moe_combine(ctab, x1, gate, rk, modsel[layer, 1, 5][None, :], ys)
    return xs[None]
```
